```python
import math
import jax, jax.numpy as jnp
from jax import lax
import numpy as np

D_MODEL = 4096
BATCH = 1
SEQ = 16384
DEPTH = 4

CHUNK = 64
Q_BLOCK = 128
D_MIX = D_MODEL

GLA_HEADS = 4
GLA_DV = D_MIX // 2 // GLA_HEADS
GLA_DK = GLA_DV // 2
GLA_GATE_RANK = 16
GLA_TAU = 16.0

DIFF_HEADS = 4
DIFF_DV = D_MIX // 4 // DIFF_HEADS
DIFF_DQK = DIFF_DV // 2

SPA_HEADS = 8
SPA_DV = D_MIX // 4 // SPA_HEADS
SPA_LAT = 256
IDX_HEADS = 16
IDX_DIM = 64
IDX_TOPK_MAX = 256

LN_EPS = 1e-5
RMS_EPS = 1e-6

IN_WIDTHS = (
    GLA_HEADS * GLA_DK,
    GLA_HEADS * GLA_DK,
    GLA_HEADS * GLA_DV,
    GLA_GATE_RANK,
    GLA_HEADS * GLA_DV,
    DIFF_HEADS * 2 * DIFF_DQK,
    DIFF_HEADS * 2 * DIFF_DQK,
    DIFF_HEADS * DIFF_DV,
    DIFF_HEADS * DIFF_DV,
    SPA_HEADS * SPA_LAT,
    SPA_LAT,
    IDX_HEADS * IDX_DIM,
    IDX_DIM,
    IDX_HEADS,
    SPA_HEADS * SPA_DV,
)
N_IN = sum(IN_WIDTHS)

kernel_name = "hymba_gla_diff_dsa_deepnorm_trunk"


def _split_points():
    pts, acc = [], 0
    for w in IN_WIDTHS[:-1]:
        acc += w
        pts.append(acc)
    return pts


def _layernorm(x, g, b):
    xf = x.astype(jnp.float32)
    mu = jnp.mean(xf, axis=-1, keepdims=True)
    var = jnp.mean(jnp.square(xf - mu), axis=-1, keepdims=True)
    y = (xf - mu) * lax.rsqrt(var + LN_EPS) * g.astype(jnp.float32) + b.astype(jnp.float32)
    return y.astype(x.dtype)


def _rmsnorm(x, g):
    xf = x.astype(jnp.float32)
    y = xf * lax.rsqrt(jnp.mean(jnp.square(xf), axis=-1, keepdims=True) + RMS_EPS) * g.astype(jnp.float32)
    return y.astype(x.dtype)


def _to_blocks(t, nb):
    b = t.shape[0]
    return jnp.moveaxis(t.reshape(b, nb, Q_BLOCK, *t.shape[2:]), 1, 0)


def _gla(q, k, v, a_lr, w_gate2, b_gate, norm_g):
    f32 = jnp.float32
    bsz, s, _ = q.shape
    nc = s // CHUNK
    shp_k = (bsz, nc, CHUNK, GLA_HEADS, GLA_DK)
    q = q.astype(f32).reshape(shp_k) * (GLA_DK ** -0.5)
    k = k.astype(f32).reshape(shp_k)
    v = v.astype(f32).reshape(bsz, nc, CHUNK, GLA_HEADS, GLA_DV)
    z = (a_lr @ w_gate2 + b_gate).astype(f32)
    log_a = (jax.nn.log_sigmoid(z) / GLA_TAU).reshape(shp_k)
    cum = jnp.cumsum(log_a, axis=2)
    total = cum[:, :, -1]
    k_dec = k * jnp.exp(total[:, :, None] - cum)

    def step(state, inp):
        q_c, k_c, v_c, tot_c = inp
        state = jnp.exp(tot_c)[..., None] * state + jnp.einsum('bchk,bchv->bhkv', k_c, v_c)
        o_c = jnp.einsum('bchk,bhkv->bchv', q_c, state)
        return state, o_c

    xs = (jnp.moveaxis(q, 1, 0), jnp.moveaxis(k_dec, 1, 0), jnp.moveaxis(v, 1, 0), jnp.moveaxis(total, 1, 0))
    s0 = jnp.zeros((bsz, GLA_HEADS, GLA_DK, GLA_DV), f32)
    _, o = lax.scan(step, s0, xs)
    o = jnp.moveaxis(o, 0, 1).reshape(bsz, s, GLA_HEADS, GLA_DV)
    o = _rmsnorm(o, norm_g)
    return o.reshape(bsz, s, GLA_HEADS * GLA_DV)


def _diff_attn(q, k, v, lam_params, norm_g, lam_init):
    bsz, s, _ = q.shape
    nb = s // Q_BLOCK
    q = q.reshape(bsz, s, DIFF_HEADS, 2, DIFF_DQK)
    k = k.reshape(bsz, s, DIFF_HEADS, 2, DIFF_DQK)
    v = v.reshape(bsz, s, DIFF_HEADS, DIFF_DV)
    lp = lam_params.astype(jnp.float32)
    lam = jnp.exp(jnp.sum(lp[0] * lp[1])) - jnp.exp(jnp.sum(lp[2] * lp[3])) + lam_init
    k_chunk = jnp.arange(s) // CHUNK
    scale = DIFF_DQK ** -0.5

    def block(args):
        q_blk, i = args
        q_chunk = (i * Q_BLOCK + jnp.arange(Q_BLOCK)) // CHUNK
        mask = k_chunk[None, :] <= q_chunk[:, None]
        sc = jnp.einsum('bqhmd,bkhmd->bhmqk', q_blk, k).astype(jnp.float32) * scale
        p = jax.nn.softmax(jnp.where(mask, sc, -jnp.inf), axis=-1)
        a = p[:, :, 0] - lam * p[:, :, 1]
        return jnp.einsum('bhqk,bkhd->bqhd', a.astype(v.dtype), v)

    o = lax.map(block, (_to_blocks(q, nb), jnp.arange(nb)))
    o = jnp.moveaxis(o, 0, 1).reshape(bsz, s, DIFF_HEADS, DIFF_DV)
    o = _rmsnorm(o, norm_g) * (1.0 - lam_init)
    return o.reshape(bsz, s, DIFF_HEADS * DIFF_DV)


def _sparse_attn(q, c_kv, iq, ik, iw, kv_norm_g, ik_ln_g, ik_ln_b, w_uv):
    bsz, s, _ = q.shape
    nb = s // Q_BLOCK
    topk = min(IDX_TOPK_MAX, s // 4)
    c = _rmsnorm(c_kv, kv_norm_g)
    ik = _layernorm(ik, ik_ln_g, ik_ln_b)
    q = q.reshape(bsz, s, SPA_HEADS, SPA_LAT)
    iq = iq.reshape(bsz, s, IDX_HEADS, IDX_DIM)
    iw = iw.astype(jnp.float32) * (IDX_HEADS ** -0.5)
    k_chunk = jnp.arange(s) // CHUNK

    def block(args):
        q_blk, iq_blk, iw_blk, i = args
        q_chunk = (i * Q_BLOCK + jnp.arange(Q_BLOCK)) // CHUNK
        admissible = k_chunk[None, :] <= q_chunk[:, None]
        rel = jax.nn.relu(jnp.einsum('bqhd,bsd->bqhs', iq_blk, ik).astype(jnp.float32)) * (IDX_DIM ** -0.5)
        score = jnp.einsum('bqhs,bqh->bqs', rel, iw_blk)
        score = jnp.where(admissible[None], score, -jnp.inf)
        _, idx = lax.top_k(score, topk)
        valid = k_chunk[idx] <= q_chunk[None, :, None]
        c_sel = jax.vmap(lambda cb, ib: cb[ib])(c, idx)
        sc = jnp.einsum('bqhd,bqkd->bqhk', q_blk, c_sel).astype(jnp.float32) * (SPA_LAT ** -0.5)
        p = jax.nn.softmax(jnp.where(valid[:, :, None, :], sc, -jnp.inf), axis=-1).astype(c.dtype)
        o_lat = jnp.einsum('bqhk,bqkd->bqhd', p, c_sel)
        return jnp.einsum('bqhd,hde->bqhe', o_lat, w_uv)

    o = lax.map(block, (_to_blocks(q, nb), _to_blocks(iq, nb), _to_blocks(iw, nb), jnp.arange(nb)))
    o = jnp.moveaxis(o, 0, 1)
    return o.reshape(bsz, s, SPA_HEADS * SPA_DV)


def setup_inputs(seed: int = 0) -> dict:
    key = jax.random.key(seed)
    ks = jax.random.split(key, 16)
    f32 = jnp.float32
    beta = (8.0 * DEPTH) ** -0.25
    nrm = lambda k, shp, sc: jax.random.normal(k, shp, f32) * sc
    return {
        "x": nrm(ks[0], (BATCH, SEQ, D_MODEL), 1.0),
        "w_in": nrm(ks[1], (DEPTH, D_MODEL, N_IN), D_MODEL ** -0.5),
        "w_out": nrm(ks[2], (DEPTH, D_MIX, D_MODEL), beta * D_MIX ** -0.5),
        "gla_w_gate2": nrm(ks[3], (DEPTH, GLA_GATE_RANK, GLA_HEADS * GLA_DK), GLA_GATE_RANK ** -0.5),
        "gla_b_gate": nrm(ks[4], (DEPTH, GLA_HEADS * GLA_DK), 0.1),
        "gla_norm_g": 1.0 + nrm(ks[5], (DEPTH, GLA_DV), 0.02),
        "diff_lambda": nrm(ks[6], (DEPTH, 4, DIFF_DQK), 0.1),
        "diff_norm_g": 1.0 + nrm(ks[7], (DEPTH, DIFF_DV), 0.02),
        "spa_kv_norm_g": 1.0 + nrm(ks[8], (DEPTH, SPA_LAT), 0.02),
        "spa_ik_ln_g": 1.0 + nrm(ks[9], (DEPTH, IDX_DIM), 0.02),
        "spa_ik_ln_b": nrm(ks[10], (DEPTH, IDX_DIM), 0.02),
        "spa_w_uv": nrm(ks[11], (DEPTH, SPA_HEADS, SPA_LAT, SPA_DV), SPA_LAT ** -0.5),
        "post_ln_g": 1.0 + nrm(ks[12], (DEPTH, D_MODEL), 0.02),
        "post_ln_b": nrm(ks[13], (DEPTH, D_MODEL), 0.02),
    }


def reference(x, w_in, w_out, gla_w_gate2, gla_b_gate, gla_norm_g, diff_lambda, diff_norm_g,
              spa_kv_norm_g, spa_ik_ln_g, spa_ik_ln_b, spa_w_uv, post_ln_g, post_ln_b):
    alpha = (2.0 * DEPTH) ** 0.25
    pts = _split_points()
    for l in range(DEPTH):
        h = x @ w_in[l]
        (aq, ak, av, aa, ag, bq, bk, bv, bg, cq, ckv, ciq, cik, ciw, cg) = jnp.split(h, pts, axis=-1)
        o_a = _gla(aq, ak, av, aa, gla_w_gate2[l], gla_b_gate[l], gla_norm_g[l])
        o_a = (o_a.astype(jnp.float32) * jax.nn.silu(ag.astype(jnp.float32))).astype(x.dtype)
        lam_init = 0.8 - 0.6 * math.exp(-0.3 * l)
        o_b = _diff_attn(bq, bk, bv, diff_lambda[l], diff_norm_g[l], lam_init)
        o_b = (o_b.astype(jnp.float32) * jax.nn.silu(bg.astype(jnp.float32))).astype(x.dtype)
        o_c = _sparse_attn(cq, ckv, ciq, cik, ciw, spa_kv_norm_g[l], spa_ik_ln_g[l], spa_ik_ln_b[l], spa_w_uv[l])
        o_c = (o_c.astype(jnp.float32) * jax.nn.silu(cg.astype(jnp.float32))).astype(x.dtype)
        y = jnp.concatenate([o_a, o_b, o_c], axis=-1) @ w_out[l]
        x = _layernorm(alpha * x + y, post_ln_g[l], post_ln_b[l])
    return x
```

```python
import functools
import math

import jax
import jax.numpy as jnp
from jax import lax
from jax.experimental import pallas as pl
from jax.experimental.pallas import tpu as pltpu

F32 = jnp.float32
BF16 = jnp.bfloat16
I32 = jnp.int32

CHUNK = 64
GLA_HEADS, GLA_DK, GLA_DV, GLA_RANK, GLA_TAU = 4, 256, 512, 16, 16.0
DIFF_HEADS, DIFF_DQK, DIFF_DV = 4, 128, 256
SPA_HEADS, SPA_DV, SPA_LAT = 8, 128, 256
IDX_HEADS, IDX_DIM, IDX_TOPK_MAX = 16, 64, 256
LN_EPS, RMS_EPS = 1e-5, 1e-6

_IN_WIDTHS = (1024, 1024, 2048, 16, 2048, 1024, 1024, 1024, 1024, 2048, 256, 1024, 64, 16, 1024)
_IN_NAMES = ("aq", "ak", "av", "aa", "ag", "bq", "bk", "bv", "bg", "cq", "ckv", "ciq", "cik", "ciw", "cg")
_IN_OFF = {}
_acc = 0
for _n, _w in zip(_IN_NAMES, _IN_WIDTHS):
    _IN_OFF[_n] = (_acc, _w)
    _acc += _w

_HB_ORDER = ("av", "cq", "ag", "aq", "bq", "bk", "bv", "ciq", "bg", "cg")
_HB_OFF = {}
_acc = 0
for _n in _HB_ORDER:
    _HB_OFF[_n] = _acc
    _acc += _IN_OFF[_n][1]
HB_WIDTH = _acc
HF_AK, HF_CKV, HF_SMALL, HF_WIDTH = 0, 1024, 1280, 1536
SM_CIK, SM_AA, SM_CIW = 0, 64, 80

VMEM_LIMIT = 56 * 1024 * 1024
NEG_BIG = -1e30
INT_MIN = -(2 ** 31)


def _cparams(sem):
    return pltpu.CompilerParams(dimension_semantics=sem, vmem_limit_bytes=VMEM_LIMIT)


def _silu(g):
    return g / (1.0 + jnp.exp(-g))


def _matmul_kernel(x_ref, w_ref, o_ref):
    o_ref[...] = jnp.dot(x_ref[...], w_ref[...], preferred_element_type=F32).astype(o_ref.dtype)


def _matmul(x, w, out_dtype, tm, tn):
    m, k = x.shape
    n = w.shape[1]
    return pl.pallas_call(
        _matmul_kernel,
        out_shape=jax.ShapeDtypeStruct((m, n), out_dtype),
        grid=(m // tm, n // tn),
        in_specs=[pl.BlockSpec((tm, k), lambda i, j: (i, 0)),
                  pl.BlockSpec((k, tn), lambda i, j: (0, j))],
        out_specs=pl.BlockSpec((tm, tn), lambda i, j: (i, j)),
        compiler_params=_cparams(("parallel", "parallel")),
        name="in_proj",
    )(x, w)


def _prep_w_in(w):
    def cols(name):
        o, n = _IN_OFF[name]
        return w[:, o:o + n]
    d = w.shape[0]
    wb = jnp.concatenate([cols(n) for n in _HB_ORDER], axis=1).astype(BF16)
    small = jnp.concatenate([cols("cik"), cols("aa"), cols("ciw"), jnp.zeros((d, 32), w.dtype)], axis=1)
    wf = jnp.concatenate([cols("ak"), cols("ckv"), small, jnp.zeros((d, HF_WIDTH - HF_SMALL - 128), w.dtype)],
                         axis=1).astype(BF16)
    return wb, wf


def _gla_kernel(v_ref, q_ref, g_ref, k_ref, sm_ref, w2_ref, b_ref, ng_ref, o_ref, st_ref, *, nchunk):
    @pl.when(pl.program_id(0) == 0)
    def _():
        st_ref[...] = jnp.zeros_like(st_ref)

    row = lax.broadcasted_iota(I32, (CHUNK, CHUNK), 0)
    col = lax.broadcasted_iota(I32, (CHUNK, CHUNK), 1)
    tri = (col <= row).astype(F32)
    for c in range(nchunk):
        rs = slice(c * CHUNK, (c + 1) * CHUNK)
        aa = sm_ref[rs, SM_AA:SM_AA + GLA_RANK].astype(BF16)
        for h in range(GLA_HEADS):
            ks = slice(h * GLA_DK, (h + 1) * GLA_DK)
            vs = slice(h * GLA_DV, (h + 1) * GLA_DV)
            z = jnp.dot(aa, w2_ref[:, ks].astype(BF16), preferred_element_type=F32) + b_ref[:, ks]
            log_a = (jnp.minimum(z, 0.0) - jnp.log1p(jnp.exp(-jnp.abs(z)))) * (1.0 / GLA_TAU)
            cum = jnp.dot(tri, log_a, preferred_element_type=F32, precision=lax.Precision.HIGHEST)
            tot = cum[CHUNK - 1:CHUNK, :]
            k_dec = (k_ref[rs, ks] * jnp.exp(tot - cum)).astype(BF16)
            upd = lax.dot_general(v_ref[rs, vs], k_dec, (((0,), (0,)), ((), ())), preferred_element_type=F32)
            st = st_ref[h] * jnp.exp(tot) + upd
            st_ref[h] = st
            o = lax.dot_general(q_ref[rs, ks], st.astype(BF16), (((1,), (1,)), ((), ())),
                                preferred_element_type=F32) * (GLA_DK ** -0.5)
            var = jnp.mean(o * o, axis=-1, keepdims=True)
            y = o * lax.rsqrt(var + RMS_EPS) * ng_ref[...]
            g = g_ref[rs, vs].astype(F32)
            o_ref[rs, vs] = (y * _silu(g)).astype(o_ref.dtype)


def _gla(hb, hf, w2, b, ng, tr):
    s = hb.shape[0]
    nchunk = tr // CHUNK
    return pl.pallas_call(
        functools.partial(_gla_kernel, nchunk=nchunk),
        out_shape=jax.ShapeDtypeStruct((s, GLA_HEADS * GLA_DV), BF16),
        grid=(s // tr,),
        in_specs=[
            pl.BlockSpec((tr, 2048), lambda i: (i, _HB_OFF["av"] // 2048)),
            pl.BlockSpec((tr, 1024), lambda i: (i, _HB_OFF["aq"] // 1024)),
            pl.BlockSpec((tr, 2048), lambda i: (i, _HB_OFF["ag"] // 2048)),
            pl.BlockSpec((tr, 1024), lambda i: (i, HF_AK // 1024)),
            pl.BlockSpec((tr, 128), lambda i: (i, HF_SMALL // 128)),
            pl.BlockSpec((GLA_RANK, 1024), lambda i: (0, 0)),
            pl.BlockSpec((1, 1024), lambda i: (0, 0)),
            pl.BlockSpec((1, GLA_DV), lambda i: (0, 0)),
        ],
        out_specs=pl.BlockSpec((tr, 2048), lambda i: (i, 0)),
        scratch_shapes=[pltpu.VMEM((GLA_HEADS, GLA_DV, GLA_DK), F32)],
        compiler_params=_cparams(("arbitrary",)),
        name="gla",
    )(hb, hb, hb, hf, hf, w2, b, ng)


def _diff_kernel(lam_ref, q_ref, k_ref, v_ref, g_ref, ng_ref, o_ref, acc_ref, *, t, lam_init):
    i = pl.program_id(1)
    scale = DIFF_DQK ** -0.5
    lp = lam_ref[...]
    lam = (jnp.exp(jnp.sum(lp[0:1] * lp[1:2], axis=(0, 1), keepdims=True))
           - jnp.exp(jnp.sum(lp[2:3] * lp[3:4], axis=(0, 1), keepdims=True)) + lam_init)
    acc_ref[...] = jnp.zeros_like(acc_ref)
    qs = [q_ref[:, 0:DIFF_DQK], q_ref[:, DIFF_DQK:2 * DIFF_DQK]]

    def tile(j, carry, masked):
        off = pl.multiple_of(j * t, t)
        kt = k_ref[pl.ds(off, t), :]
        vt = v_ref[pl.ds(off, t), :]
        out = []
        for mp in range(2):
            m_old, l_old = carry[mp]
            s = lax.dot_general(qs[mp], kt[:, mp * DIFF_DQK:(mp + 1) * DIFF_DQK], (((1,), (1,)), ((), ())),
                                preferred_element_type=F32) * scale
            if masked:
                rch = lax.broadcasted_iota(I32, (t, t), 0) // CHUNK
                cch = lax.broadcasted_iota(I32, (t, t), 1) // CHUNK
                s = jnp.where(cch <= rch, s, NEG_BIG)
            m_new = jnp.maximum(m_old, jnp.max(s, axis=-1, keepdims=True))
            p = jnp.exp(s - m_new)
            alpha = jnp.exp(m_old - m_new)
            l_new = alpha * l_old + jnp.sum(p, axis=-1, keepdims=True)
            acc_ref[mp] = acc_ref[mp] * alpha + jnp.dot(p.astype(BF16), vt, preferred_element_type=F32)
            out.append((m_new, l_new))
        return tuple(out)

    init = tuple((jnp.full((t, 1), NEG_BIG, F32), jnp.zeros((t, 1), F32)) for _ in range(2))
    carry = lax.fori_loop(0, i, lambda j, c: tile(j, c, False), init)
    (_, l0), (_, l1) = tile(i, carry, True)
    o = acc_ref[0] / l0 - lam * (acc_ref[1] / l1)
    var = jnp.mean(o * o, axis=-1, keepdims=True)
    y = o * lax.rsqrt(var + RMS_EPS) * ng_ref[...] * (1.0 - lam_init)
    g = g_ref[...].astype(F32)
    o_ref[...] = (y * _silu(g)).astype(o_ref.dtype)


def _diff(hb, lam_p, ng, lam_init, t):
    s = hb.shape[0]
    w = DIFF_DV
    return pl.pallas_call(
        functools.partial(_diff_kernel, t=t, lam_init=lam_init),
        out_shape=jax.ShapeDtypeStruct((s, DIFF_HEADS * DIFF_DV), BF16),
        grid=(DIFF_HEADS, s // t),
        in_specs=[
            pl.BlockSpec((4, DIFF_DQK), lambda h, i: (0, 0)),
            pl.BlockSpec((t, w), lambda h, i: (i, _HB_OFF["bq"] // w + h)),
            pl.BlockSpec((s, w), lambda h, i: (0, _HB_OFF["bk"] // w + h)),
            pl.BlockSpec((s, w), lambda h, i: (0, _HB_OFF["bv"] // w + h)),
            pl.BlockSpec((t, w), lambda h, i: (i, _HB_OFF["bg"] // w + h)),
            pl.BlockSpec((1, DIFF_DV), lambda h, i: (0, 0)),
        ],
        out_specs=pl.BlockSpec((t, w), lambda h, i: (i, h)),
        scratch_shapes=[pltpu.VMEM((2, t, DIFF_DV), F32)],
        compiler_params=_cparams(("parallel", "arbitrary")),
        name="diff_attn",
    )(lam_p, hb, hb, hb, hb, ng)


def _spa_norm_kernel(ckv_ref, sm_ref, kvg_ref, lg_ref, lb_ref, cn_ref, cnt_ref, ikn_ref):
    c = ckv_ref[...]
    cn = c * lax.rsqrt(jnp.mean(c * c, axis=-1, keepdims=True) + RMS_EPS) * kvg_ref[...]
    cn_ref[...] = cn.astype(cn_ref.dtype)
    cnt_ref[...] = cn.T.astype(cnt_ref.dtype)
    ik = sm_ref[:, SM_CIK:SM_CIK + IDX_DIM]
    mu = jnp.mean(ik, axis=-1, keepdims=True)
    d = ik - mu
    var = jnp.mean(d * d, axis=-1, keepdims=True)
    ikn_ref[...] = (d * lax.rsqrt(var + LN_EPS) * lg_ref[...] + lb_ref[...]).astype(ikn_ref.dtype)


def _spa_norm(hf, kvg, lg, lb, tk):
    s = hf.shape[0]
    nt = s // tk
    return pl.pallas_call(
        _spa_norm_kernel,
        out_shape=(jax.ShapeDtypeStruct((nt, tk, SPA_LAT), BF16),
                   jax.ShapeDtypeStruct((nt, SPA_LAT, tk), BF16),
                   jax.ShapeDtypeStruct((nt, tk, IDX_DIM), BF16)),
        grid=(nt,),
        in_specs=[
            pl.BlockSpec((tk, SPA_LAT), lambda i: (i, HF_CKV // SPA_LAT)),
            pl.BlockSpec((tk, 128), lambda i: (i, HF_SMALL // 128)),
            pl.BlockSpec((1, SPA_LAT), lambda i: (0, 0)),
            pl.BlockSpec((1, IDX_DIM), lambda i: (0, 0)),
            pl.BlockSpec((1, IDX_DIM), lambda i: (0, 0)),
        ],
        out_specs=(pl.BlockSpec((None, tk, SPA_LAT), lambda i: (i, 0, 0)),
                   pl.BlockSpec((None, SPA_LAT, tk), lambda i: (i, 0, 0)),
                   pl.BlockSpec((None, tk, IDX_DIM), lambda i: (i, 0, 0))),
        compiler_params=_cparams(("parallel",)),
        name="spa_norm",
    )(hf, hf, kvg, lg, lb)


def _spa_kernel(iqt_ref, q8t_ref, w_ref, g_ref, ikn_ref, cn_ref, cnt_ref, wuv_ref, o_ref, keys_ref, acc_ref,
                *, tq, tk, topk):
    i = pl.program_id(0)
    n_t = (i * tq + tq + tk - 1) // tk
    qpos = i * tq + lax.broadcasted_iota(I32, (1, tq), 1)
    qch = qpos // CHUNK
    k_row = jnp.minimum(topk, (qch + 1) * CHUNK)

    def body_a(j, _):
        ikt = ikn_ref[j]
        sc = jnp.zeros((tk, tq), F32)
        for hp in range(IDX_HEADS // 2):
            r = jnp.dot(ikt, iqt_ref[:, hp * 2 * tq:(hp + 1) * 2 * tq], preferred_element_type=F32)
            for hh in range(2):
                h = hp * 2 + hh
                sc = sc + jnp.maximum(r[:, hh * tq:(hh + 1) * tq], 0.0) * w_ref[h:h + 1, :]
        kch = (j * tk + lax.broadcasted_iota(I32, (tk, tq), 0)) // CHUNK
        bits = pltpu.bitcast(sc, I32)
        key = bits ^ ((bits >> 31) & 0x7FFFFFFF)
        keys_ref[j] = jnp.where(kch <= qch, key, INT_MIN)
        return 0

    lax.fori_loop(0, n_t, body_a, 0)

    def count_ge(cand):
        def body(j, cnt):
            ind = (keys_ref[j] >= cand).astype(I32)
            return cnt + jnp.sum(ind.reshape(tk // 8, 8, tq), axis=0)
        cnt = lax.fori_loop(0, n_t, body, jnp.zeros((8, tq), I32))
        return jnp.sum(cnt, axis=0, keepdims=True)

    def bit_body(b, a):
        cand_a = a | lax.shift_left(jnp.int32(1), 31 - b)
        cnt = count_ge(cand_a ^ INT_MIN)
        return jnp.where(cnt >= k_row, cand_a, a)

    thr = lax.fori_loop(0, 32, bit_body, jnp.zeros((1, tq), I32)) ^ INT_MIN

    acc_ref[...] = jnp.zeros_like(acc_ref)
    scale = SPA_LAT ** -0.5

    def body_c(j, carry):
        m_old, l_old = carry
        st = jnp.dot(cn_ref[j], q8t_ref[...], preferred_element_type=F32) * scale
        madd = jnp.where(keys_ref[j] >= thr, 0.0, NEG_BIG)
        s = st + jnp.concatenate([madd] * SPA_HEADS, axis=1)
        m_new = jnp.maximum(m_old, jnp.max(s, axis=0, keepdims=True))
        p = jnp.exp(s - m_new)
        alpha = jnp.exp(m_old - m_new)
        l_new = alpha * l_old + jnp.sum(p, axis=0, keepdims=True)
        acc_ref[...] = acc_ref[...] * alpha + jnp.dot(cnt_ref[j], p.astype(BF16), preferred_element_type=F32)
        return m_new, l_new

    init = (jnp.full((1, SPA_HEADS * tq), NEG_BIG, F32), jnp.zeros((1, SPA_HEADS * tq), F32))
    _, l_fin = lax.fori_loop(0, n_t, body_c, init)

    for h in range(SPA_HEADS):
        cs = slice(h * tq, (h + 1) * tq)
        o_lat = (acc_ref[:, cs] / l_fin[:, cs]).T
        oh = jnp.dot(o_lat.astype(BF16), wuv_ref[h], preferred_element_type=F32)
        g = g_ref[:, h * SPA_DV:(h + 1) * SPA_DV].astype(F32)
        o_ref[:, h * SPA_DV:(h + 1) * SPA_DV] = (oh * _silu(g)).astype(o_ref.dtype)


def _spa(hb, iqt, q8t, wt, ikn, cn, cnt, wuv, tq, tk, topk):
    s = hb.shape[0]
    nq, nt = s // tq, s // tk
    const3 = lambda i: (0, 0, 0)
    return pl.pallas_call(
        functools.partial(_spa_kernel, tq=tq, tk=tk, topk=topk),
        out_shape=jax.ShapeDtypeStruct((s, SPA_HEADS * SPA_DV), BF16),
        grid=(nq,),
        in_specs=[
            pl.BlockSpec((None, IDX_DIM, IDX_HEADS * tq), lambda i: (i, 0, 0)),
            pl.BlockSpec((None, SPA_LAT, SPA_HEADS * tq), lambda i: (i, 0, 0)),
            pl.BlockSpec((None, IDX_HEADS, tq), lambda i: (i, 0, 0)),
            pl.BlockSpec((tq, 1024), lambda i: (i, _HB_OFF["cg"] // 1024)),
            pl.BlockSpec((nt, tk, IDX_DIM), const3, pipeline_mode=pl.Buffered(1)),
            pl.BlockSpec((nt, tk, SPA_LAT), const3, pipeline_mode=pl.Buffered(1)),
            pl.BlockSpec((nt, SPA_LAT, tk), const3, pipeline_mode=pl.Buffered(1)),
            pl.BlockSpec((SPA_HEADS, SPA_LAT, SPA_DV), const3, pipeline_mode=pl.Buffered(1)),
        ],
        out_specs=pl.BlockSpec((tq, 1024), lambda i: (i, 0)),
        scratch_shapes=[pltpu.VMEM((nt, tk, tq), I32), pltpu.VMEM((SPA_LAT, SPA_HEADS * tq), F32)],
        compiler_params=_cparams(("arbitrary",)),
        name="sparse_attn",
    )(iqt, q8t, wt, hb, ikn, cn, cnt, wuv)


def _out_kernel(oa_ref, ob_ref, oc_ref, x_ref, w_ref, lg_ref, lb_ref, y_ref, yb_ref, acc_ref, *, alpha, nj, tn):
    j = pl.program_id(1)
    na, nb = oa_ref.shape[1], ob_ref.shape[1]
    acc_ref[j] = (jnp.dot(oa_ref[...], w_ref[0:na, :], preferred_element_type=F32)
                  + jnp.dot(ob_ref[...], w_ref[na:na + nb, :], preferred_element_type=F32)
                  + jnp.dot(oc_ref[...], w_ref[na + nb:, :], preferred_element_type=F32))

    @pl.when(j == nj - 1)
    def _():
        d = nj * tn
        ssum = 0.0
        for jj in range(nj):
            r = alpha * x_ref[:, jj * tn:(jj + 1) * tn] + acc_ref[jj]
            acc_ref[jj] = r
            ssum = ssum + jnp.sum(r, axis=-1, keepdims=True)
        mu = ssum / d
        vsum = 0.0
        for jj in range(nj):
            c = acc_ref[jj] - mu
            vsum = vsum + jnp.sum(c * c, axis=-1, keepdims=True)
        rstd = lax.rsqrt(vsum / d + LN_EPS)
        for jj in range(nj):
            cs = slice(jj * tn, (jj + 1) * tn)
            y = (acc_ref[jj] - mu) * rstd * lg_ref[:, cs] + lb_ref[:, cs]
            y_ref[:, cs] = y
            yb_ref[:, cs] = y.astype(BF16)


def _out_proj(oa, ob, oc, x, w, lg, lb, alpha, tm, tn):
    s, d = x.shape
    nj = d // tn
    row = lambda i, j: (i, 0)
    return pl.pallas_call(
        functools.partial(_out_kernel, alpha=alpha, nj=nj, tn=tn),
        out_shape=(jax.ShapeDtypeStruct((s, d), F32), jax.ShapeDtypeStruct((s, d), BF16)),
        grid=(s // tm, nj),
        in_specs=[
            pl.BlockSpec((tm, oa.shape[1]), row),
            pl.BlockSpec((tm, ob.shape[1]), row),
            pl.BlockSpec((tm, oc.shape[1]), row),
            pl.BlockSpec((tm, d), row),
            pl.BlockSpec((w.shape[0], tn), lambda i, j: (0, j)),
            pl.BlockSpec((1, d), lambda i, j: (0, 0)),
            pl.BlockSpec((1, d), lambda i, j: (0, 0)),
        ],
        out_specs=(pl.BlockSpec((tm, d), row), pl.BlockSpec((tm, d), row)),
        scratch_shapes=[pltpu.VMEM((nj, tm, tn), F32)],
        compiler_params=_cparams(("parallel", "arbitrary")),
        name="out_proj_ln",
    )(oa, ob, oc, x, w, lg, lb)


def kernel(x, w_in, w_out, gla_w_gate2, gla_b_gate, gla_norm_g, diff_lambda, diff_norm_g,
           spa_kv_norm_g, spa_ik_ln_g, spa_ik_ln_b, spa_w_uv, post_ln_g, post_ln_b):
    bsz, s, d = x.shape
    assert bsz == 1 and d == 4096 and w_in.shape[2] == sum(_IN_WIDTHS)
    depth = w_in.shape[0]
    alpha = (2.0 * depth) ** 0.25
    topk = min(IDX_TOPK_MAX, s // 4)
    tm_in = min(1024, s)
    t_diff = min(512, s)
    tq_s, tk_s = 128, min(512, s)
    nq_s = s // tq_s

    xf = x[0]
    xb = xf.astype(BF16)
    for l in range(depth):
        wb, wf = _prep_w_in(w_in[l])
        hb = _matmul(xb, wb, BF16, tm_in, 512)
        hf = _matmul(xb, wf, F32, tm_in, 512)

        o_a = _gla(hb, hf, gla_w_gate2[l], gla_b_gate[l][None], gla_norm_g[l][None], 128)

        lam_init = 0.8 - 0.6 * math.exp(-0.3 * l)
        o_b = _diff(hb, diff_lambda[l], diff_norm_g[l][None], lam_init, t_diff)

        cn, cnt, ikn = _spa_norm(hf, spa_kv_norm_g[l][None], spa_ik_ln_g[l][None], spa_ik_ln_b[l][None], tk_s)
        ciq = lax.slice_in_dim(hb, _HB_OFF["ciq"], _HB_OFF["ciq"] + 1024, axis=1)
        iqt = ciq.reshape(nq_s, tq_s, IDX_HEADS, IDX_DIM).transpose(0, 3, 2, 1).reshape(nq_s, IDX_DIM, IDX_HEADS * tq_s)
        cq = lax.slice_in_dim(hb, _HB_OFF["cq"], _HB_OFF["cq"] + 2048, axis=1)
        q8t = cq.reshape(nq_s, tq_s, SPA_HEADS, SPA_LAT).transpose(0, 3, 2, 1).reshape(nq_s, SPA_LAT, SPA_HEADS * tq_s)
        ciw = lax.slice_in_dim(hf, HF_SMALL + SM_CIW, HF_SMALL + SM_CIW + IDX_HEADS, axis=1)
        wt = (ciw * (IDX_HEADS ** -0.5 * IDX_DIM ** -0.5)).reshape(nq_s, tq_s, IDX_HEADS).transpose(0, 2, 1)
        o_c = _spa(hb, iqt, q8t, wt, ikn, cn, cnt, spa_w_uv[l].astype(BF16), tq_s, tk_s, topk)

        xf, xb = _out_proj(o_a, o_b, o_c, xf, w_out[l].astype(BF16), post_ln_g[l][None], post_ln_b[l][None],
                           alpha, 256, 1024)
    return xf[None]
```

```python
import functools
import math

import jax
import jax.numpy as jnp
from jax import lax
from jax.experimental import pallas as pl
from jax.experimental.pallas import tpu as pltpu

F32 = jnp.float32
BF16 = jnp.bfloat16
I32 = jnp.int32

CHUNK = 64
GLA_HEADS, GLA_DK, GLA_DV, GLA_RANK, GLA_TAU = 4, 256, 512, 16, 16.0
DIFF_HEADS, DIFF_DQK, DIFF_DV = 4, 128, 256
SPA_HEADS, SPA_DV, SPA_LAT = 8, 128, 256
IDX_HEADS, IDX_DIM, IDX_TOPK_MAX = 16, 64, 256
LN_EPS, RMS_EPS = 1e-5, 1e-6

_IN_WIDTHS = (1024, 1024, 2048, 16, 2048, 1024, 1024, 1024, 1024, 2048, 256, 1024, 64, 16, 1024)
_IN_NAMES = ("aq", "ak", "av", "aa", "ag", "bq", "bk", "bv", "bg", "cq", "ckv", "ciq", "cik", "ciw", "cg")
_IN_OFF = {}
_acc = 0
for _n, _w in zip(_IN_NAMES, _IN_WIDTHS):
    _IN_OFF[_n] = (_acc, _w)
    _acc += _w

_HB_ORDER = ("av", "cq", "ag", "aq", "bq", "bk", "bv", "ciq", "bg", "cg")
_HB_OFF = {}
_acc = 0
for _n in _HB_ORDER:
    _HB_OFF[_n] = _acc
    _acc += _IN_OFF[_n][1]
HB_WIDTH = _acc
HF_AK, HF_CKV, HF_SMALL, HF_WIDTH = 0, 1024, 1280, 1536
SM_CIK, SM_AA, SM_CIW = 0, 64, 80

DIFF_RC = 128
VMEM_LIMIT = 56 * 1024 * 1024
NEG_BIG = -1e30
INT_MIN = -(2 ** 31)


def _cparams(sem):
    return pltpu.CompilerParams(dimension_semantics=sem, vmem_limit_bytes=VMEM_LIMIT)


def _silu(g):
    return g / (1.0 + jnp.exp(-g))


def _matmul_kernel(x_ref, w_ref, o_ref):
    o_ref[...] = jnp.dot(x_ref[...], w_ref[...], preferred_element_type=F32).astype(o_ref.dtype)


def _matmul(x, w, out_dtype, tm, tn):
    m, k = x.shape
    n = w.shape[1]
    return pl.pallas_call(
        _matmul_kernel,
        out_shape=jax.ShapeDtypeStruct((m, n), out_dtype),
        grid=(m // tm, n // tn),
        in_specs=[pl.BlockSpec((tm, k), lambda i, j: (i, 0)),
                  pl.BlockSpec((k, tn), lambda i, j: (0, j))],
        out_specs=pl.BlockSpec((tm, tn), lambda i, j: (i, j)),
        compiler_params=_cparams(("parallel", "parallel")),
        name="in_proj",
    )(x, w)


def _prep_w_in(w):
    def cols(name):
        o, n = _IN_OFF[name]
        return w[:, o:o + n]
    d = w.shape[0]
    wb = jnp.concatenate([cols(n) for n in _HB_ORDER], axis=1).astype(BF16)
    small = jnp.concatenate([cols("cik"), cols("aa"), cols("ciw"), jnp.zeros((d, 32), w.dtype)], axis=1)
    wf = jnp.concatenate([cols("ak"), cols("ckv"), small, jnp.zeros((d, HF_WIDTH - HF_SMALL - 128), w.dtype)],
                         axis=1).astype(BF16)
    return wb, wf


def _gla_kernel(v_ref, q_ref, g_ref, k_ref, sm_ref, w2_ref, b_ref, ng_ref, o_ref, st_ref, *, nchunk):
    @pl.when(pl.program_id(0) == 0)
    def _():
        st_ref[...] = jnp.zeros_like(st_ref)

    row = lax.broadcasted_iota(I32, (CHUNK, CHUNK), 0)
    col = lax.broadcasted_iota(I32, (CHUNK, CHUNK), 1)
    tri = (col <= row).astype(F32)
    for c in range(nchunk):
        rs = slice(c * CHUNK, (c + 1) * CHUNK)
        aa = sm_ref[rs, SM_AA:SM_AA + GLA_RANK].astype(BF16)
        for h in range(GLA_HEADS):
            ks = slice(h * GLA_DK, (h + 1) * GLA_DK)
            vs = slice(h * GLA_DV, (h + 1) * GLA_DV)
            z = jnp.dot(aa, w2_ref[:, ks].astype(BF16), preferred_element_type=F32) + b_ref[:, ks]
            log_a = (jnp.minimum(z, 0.0) - jnp.log1p(jnp.exp(-jnp.abs(z)))) * (1.0 / GLA_TAU)
            cum = jnp.dot(tri, log_a, preferred_element_type=F32, precision=lax.Precision.HIGHEST)
            tot = cum[CHUNK - 1:CHUNK, :]
            k_dec = (k_ref[rs, ks] * jnp.exp(tot - cum)).astype(BF16)
            upd = lax.dot_general(v_ref[rs, vs], k_dec, (((0,), (0,)), ((), ())), preferred_element_type=F32)
            st = st_ref[h] * jnp.exp(tot) + upd
            st_ref[h] = st
            o = lax.dot_general(q_ref[rs, ks], st.astype(BF16), (((1,), (1,)), ((), ())),
                                preferred_element_type=F32) * (GLA_DK ** -0.5)
            var = jnp.mean(o * o, axis=-1, keepdims=True)
            y = o * lax.rsqrt(var + RMS_EPS) * ng_ref[...]
            g = g_ref[rs, vs].astype(F32)
            o_ref[rs, vs] = (y * _silu(g)).astype(o_ref.dtype)


def _gla(hb, hf, w2, b, ng, tr):
    s = hb.shape[0]
    nchunk = tr // CHUNK
    return pl.pallas_call(
        functools.partial(_gla_kernel, nchunk=nchunk),
        out_shape=jax.ShapeDtypeStruct((s, GLA_HEADS * GLA_DV), BF16),
        grid=(s // tr,),
        in_specs=[
            pl.BlockSpec((tr, 2048), lambda i: (i, _HB_OFF["av"] // 2048)),
            pl.BlockSpec((tr, 1024), lambda i: (i, _HB_OFF["aq"] // 1024)),
            pl.BlockSpec((tr, 2048), lambda i: (i, _HB_OFF["ag"] // 2048)),
            pl.BlockSpec((tr, 1024), lambda i: (i, HF_AK // 1024)),
            pl.BlockSpec((tr, 128), lambda i: (i, HF_SMALL // 128)),
            pl.BlockSpec((GLA_RANK, 1024), lambda i: (0, 0)),
            pl.BlockSpec((1, 1024), lambda i: (0, 0)),
            pl.BlockSpec((1, GLA_DV), lambda i: (0, 0)),
        ],
        out_specs=pl.BlockSpec((tr, 2048), lambda i: (i, 0)),
        scratch_shapes=[pltpu.VMEM((GLA_HEADS, GLA_DV, GLA_DK), F32)],
        compiler_params=_cparams(("arbitrary",)),
        name="gla",
    )(hb, hb, hb, hf, hf, w2, b, ng)


def _diff_kernel(lam_ref, q_ref, k_ref, v_ref, g_ref, ng_ref, o_ref, acc_ref, *, t, lam_init):
    i = pl.program_id(1)
    scale = DIFF_DQK ** -0.5
    lp = lam_ref[...]
    lam = (jnp.exp(jnp.sum(lp[0:1] * lp[1:2], axis=(0, 1), keepdims=True))
           - jnp.exp(jnp.sum(lp[2:3] * lp[3:4], axis=(0, 1), keepdims=True)) + lam_init)
    acc_ref[...] = jnp.zeros_like(acc_ref)
    c2 = scale * math.log2(math.e)
    rc = DIFF_RC
    nrc = t // rc

    def tile(j, carry, masked):
        off = pl.multiple_of(j * t, t)
        kt = k_ref[pl.ds(off, t), :]
        vt = v_ref[pl.ds(off, t), :]
        out = []
        for mp in range(2):
            ktm = kt[:, mp * DIFF_DQK:(mp + 1) * DIFF_DQK]
            for r in range(nrc):
                rs = slice(r * rc, (r + 1) * rc)
                m_old, l_old = carry[mp * nrc + r]
                s = lax.dot_general(q_ref[rs, mp * DIFF_DQK:(mp + 1) * DIFF_DQK], ktm, (((1,), (1,)), ((), ())),
                                    preferred_element_type=F32)
                if masked:
                    rch = (lax.broadcasted_iota(I32, (rc, t), 0) + r * rc) // CHUNK
                    cch = lax.broadcasted_iota(I32, (rc, t), 1) // CHUNK
                    s = jnp.where(cch <= rch, s, NEG_BIG)
                m_new = jnp.maximum(m_old, jnp.max(s, axis=-1, keepdims=True))
                p = jnp.exp2((s - m_new) * c2)
                alpha = jnp.exp2((m_old - m_new) * c2)
                l_new = alpha * l_old + jnp.sum(p, axis=-1, keepdims=True)
                acc_ref[mp, rs, :] = acc_ref[mp, rs, :] * alpha + jnp.dot(p.astype(BF16), vt,
                                                                         preferred_element_type=F32)
                out.append((m_new, l_new))
        return tuple(out)

    init = tuple((jnp.full((rc, 1), NEG_BIG, F32), jnp.zeros((rc, 1), F32)) for _ in range(2 * nrc))
    carry = lax.fori_loop(0, i, lambda j, c: tile(j, c, False), init)
    fin = tile(i, carry, True)
    l0 = jnp.concatenate([fin[r][1] for r in range(nrc)], axis=0)
    l1 = jnp.concatenate([fin[nrc + r][1] for r in range(nrc)], axis=0)
    o = acc_ref[0] / l0 - lam * (acc_ref[1] / l1)
    var = jnp.mean(o * o, axis=-1, keepdims=True)
    y = o * lax.rsqrt(var + RMS_EPS) * ng_ref[...] * (1.0 - lam_init)
    g = g_ref[...].astype(F32)
    o_ref[...] = (y * _silu(g)).astype(o_ref.dtype)


def _diff(hb, lam_p, ng, lam_init, t):
    s = hb.shape[0]
    w = DIFF_DV
    return pl.pallas_call(
        functools.partial(_diff_kernel, t=t, lam_init=lam_init),
        out_shape=jax.ShapeDtypeStruct((s, DIFF_HEADS * DIFF_DV), BF16),
        grid=(DIFF_HEADS, s // t),
        in_specs=[
            pl.BlockSpec((4, DIFF_DQK), lambda h, i: (0, 0)),
            pl.BlockSpec((t, w), lambda h, i: (i, _HB_OFF["bq"] // w + h)),
            pl.BlockSpec((s, w), lambda h, i: (0, _HB_OFF["bk"] // w + h)),
            pl.BlockSpec((s, w), lambda h, i: (0, _HB_OFF["bv"] // w + h)),
            pl.BlockSpec((t, w), lambda h, i: (i, _HB_OFF["bg"] // w + h)),
            pl.BlockSpec((1, DIFF_DV), lambda h, i: (0, 0)),
        ],
        out_specs=pl.BlockSpec((t, w), lambda h, i: (i, h)),
        scratch_shapes=[pltpu.VMEM((2, t, DIFF_DV), F32)],
        compiler_params=_cparams(("parallel", "arbitrary")),
        name="diff_attn",
    )(lam_p, hb, hb, hb, hb, ng)


def _spa_norm_kernel(ckv_ref, sm_ref, kvg_ref, lg_ref, lb_ref, cn_ref, cnt_ref, ikn_ref):
    c = ckv_ref[...]
    cn = c * lax.rsqrt(jnp.mean(c * c, axis=-1, keepdims=True) + RMS_EPS) * kvg_ref[...]
    cn_ref[...] = cn.astype(cn_ref.dtype)
    cnt_ref[...] = cn.T.astype(cnt_ref.dtype)
    ik = sm_ref[:, SM_CIK:SM_CIK + IDX_DIM]
    mu = jnp.mean(ik, axis=-1, keepdims=True)
    d = ik - mu
    var = jnp.mean(d * d, axis=-1, keepdims=True)
    ikn_ref[...] = (d * lax.rsqrt(var + LN_EPS) * lg_ref[...] + lb_ref[...]).astype(ikn_ref.dtype)


def _spa_norm(hf, kvg, lg, lb, tk):
    s = hf.shape[0]
    nt = s // tk
    return pl.pallas_call(
        _spa_norm_kernel,
        out_shape=(jax.ShapeDtypeStruct((nt, tk, SPA_LAT), BF16),
                   jax.ShapeDtypeStruct((nt, SPA_LAT, tk), BF16),
                   jax.ShapeDtypeStruct((nt, tk, IDX_DIM), BF16)),
        grid=(nt,),
        in_specs=[
            pl.BlockSpec((tk, SPA_LAT), lambda i: (i, HF_CKV // SPA_LAT)),
            pl.BlockSpec((tk, 128), lambda i: (i, HF_SMALL // 128)),
            pl.BlockSpec((1, SPA_LAT), lambda i: (0, 0)),
            pl.BlockSpec((1, IDX_DIM), lambda i: (0, 0)),
            pl.BlockSpec((1, IDX_DIM), lambda i: (0, 0)),
        ],
        out_specs=(pl.BlockSpec((None, tk, SPA_LAT), lambda i: (i, 0, 0)),
                   pl.BlockSpec((None, SPA_LAT, tk), lambda i: (i, 0, 0)),
                   pl.BlockSpec((None, tk, IDX_DIM), lambda i: (i, 0, 0))),
        compiler_params=_cparams(("parallel",)),
        name="spa_norm",
    )(hf, hf, kvg, lg, lb)


def _spa_kernel(iqt_ref, q8t_ref, w_ref, g_ref, ikn_ref, cn_ref, cnt_ref, wuv_ref, o_ref, keys_ref, acc_ref,
                *, tq, tk, topk, idx_bits):
    i = pl.program_id(0)
    n_t = (i * tq + tq + tk - 1) // tk
    qpos = i * tq + lax.broadcasted_iota(I32, (1, tq), 1)
    qch = qpos // CHUNK
    k_row = jnp.minimum(topk, (qch + 1) * CHUNK)
    hk = tk // 2

    def score_tile(j, masked):
        for half in range(2):
            ks = slice(half * hk, (half + 1) * hk)
            ikt = ikn_ref[j, ks, :]
            sc = jnp.zeros((hk, tq), F32)
            for hp in range(IDX_HEADS // 2):
                r = jnp.dot(ikt, iqt_ref[:, hp * 2 * tq:(hp + 1) * 2 * tq], preferred_element_type=F32)
                for hh in range(2):
                    h = hp * 2 + hh
                    sc = sc + jnp.maximum(r[:, hh * tq:(hh + 1) * tq], 0.0) * w_ref[h:h + 1, :]
            bits = pltpu.bitcast(sc, I32)
            key = bits ^ ((bits >> 31) & 0x7FFFFFFF)
            if masked:
                kch = (j * tk + half * hk + lax.broadcasted_iota(I32, (hk, tq), 0)) // CHUNK
                key = jnp.where(kch <= qch, key, INT_MIN)
            keys_ref[j, ks, :] = key

    def body_a(j, _):
        score_tile(j, False)
        return 0

    lax.fori_loop(0, n_t - 1, body_a, 0)
    score_tile(n_t - 1, True)

    def count_where(pred):
        def body(j, cnt):
            ind = jnp.where(pred(keys_ref[j], j), 1, 0).astype(I32)
            return cnt + jnp.sum(ind.reshape(tk // 8, 8, tq), axis=0)
        cnt = lax.fori_loop(0, n_t, body, jnp.zeros((8, tq), I32))
        return jnp.sum(cnt, axis=0, keepdims=True)

    def bis_cond(st):
        b, _, cnt_a = st
        return jnp.logical_and(b < 32, jnp.max(cnt_a - k_row) > 0)

    def bis_body(st):
        b, a, cnt_a = st
        cand_a = a | lax.shift_left(jnp.int32(1), 31 - b)
        cand = cand_a ^ INT_MIN
        cnt = count_where(lambda kt, j: kt >= cand)
        take = cnt >= k_row
        return b + 1, jnp.where(take, cand_a, a), jnp.where(take, cnt, cnt_a)

    _, a_fin, cnt_fin = lax.while_loop(
        bis_cond, bis_body, (jnp.int32(0), jnp.zeros((1, tq), I32), jnp.zeros((1, tq), I32) + n_t * tk))
    thr = a_fin ^ INT_MIN

    @pl.when(jnp.max(cnt_fin - k_row) > 0)
    def _():
        need = k_row - count_where(lambda kt, j: kt > thr)

        def idx_of(j):
            return j * tk + lax.broadcasted_iota(I32, (tk, tq), 0)

        def idx_body(b, y):
            cand = y | lax.shift_left(jnp.int32(1), idx_bits - 1 - b)
            cnt = count_where(lambda kt, j: jnp.logical_and(kt == thr, idx_of(j) < cand))
            return jnp.where(cnt < need, cand, y)

        y_keep = lax.fori_loop(0, idx_bits, idx_body, jnp.zeros((1, tq), I32))

        def fix_body(j, _):
            kt = keys_ref[j]
            drop = jnp.logical_and(kt == thr, idx_of(j) > y_keep)
            keys_ref[j] = jnp.where(drop, kt - 1, kt)
            return 0

        lax.fori_loop(0, n_t, fix_body, 0)

    acc_ref[...] = jnp.zeros_like(acc_ref)
    c2 = SPA_LAT ** -0.5 * math.log2(math.e)

    def body_c(j, carry):
        m_old, l_old = carry
        st = jnp.dot(cn_ref[j], q8t_ref[...], preferred_element_type=F32)
        madd = jnp.where(keys_ref[j] >= thr, 0.0, NEG_BIG)
        s = st + jnp.concatenate([madd] * SPA_HEADS, axis=1)
        m_new = jnp.maximum(m_old, jnp.max(s, axis=0, keepdims=True))
        p = jnp.exp2((s - m_new) * c2)
        alpha = jnp.exp2((m_old - m_new) * c2)
        l_new = alpha * l_old + jnp.sum(p, axis=0, keepdims=True)
        acc_ref[...] = acc_ref[...] * alpha + jnp.dot(cnt_ref[j], p.astype(BF16), preferred_element_type=F32)
        return m_new, l_new

    init = (jnp.full((1, SPA_HEADS * tq), NEG_BIG, F32), jnp.zeros((1, SPA_HEADS * tq), F32))
    _, l_fin = lax.fori_loop(0, n_t, body_c, init)

    for h in range(SPA_HEADS):
        cs = slice(h * tq, (h + 1) * tq)
        o_lat = (acc_ref[:, cs] / l_fin[:, cs]).T
        oh = jnp.dot(o_lat.astype(BF16), wuv_ref[h], preferred_element_type=F32)
        g = g_ref[:, h * SPA_DV:(h + 1) * SPA_DV].astype(F32)
        o_ref[:, h * SPA_DV:(h + 1) * SPA_DV] = (oh * _silu(g)).astype(o_ref.dtype)


def _spa(hb, iqt, q8t, wt, ikn, cn, cnt, wuv, tq, tk, topk):
    s = hb.shape[0]
    nq, nt = s // tq, s // tk
    const3 = lambda i: (0, 0, 0)
    return pl.pallas_call(
        functools.partial(_spa_kernel, tq=tq, tk=tk, topk=topk, idx_bits=(s - 1).bit_length()),
        out_shape=jax.ShapeDtypeStruct((s, SPA_HEADS * SPA_DV), BF16),
        grid=(nq,),
        in_specs=[
            pl.BlockSpec((None, IDX_DIM, IDX_HEADS * tq), lambda i: (i, 0, 0)),
            pl.BlockSpec((None, SPA_LAT, SPA_HEADS * tq), lambda i: (i, 0, 0)),
            pl.BlockSpec((None, IDX_HEADS, tq), lambda i: (i, 0, 0)),
            pl.BlockSpec((tq, 1024), lambda i: (i, _HB_OFF["cg"] // 1024)),
            pl.BlockSpec((nt, tk, IDX_DIM), const3, pipeline_mode=pl.Buffered(1)),
            pl.BlockSpec((nt, tk, SPA_LAT), const3, pipeline_mode=pl.Buffered(1)),
            pl.BlockSpec((nt, SPA_LAT, tk), const3, pipeline_mode=pl.Buffered(1)),
            pl.BlockSpec((SPA_HEADS, SPA_LAT, SPA_DV), const3, pipeline_mode=pl.Buffered(1)),
        ],
        out_specs=pl.BlockSpec((tq, 1024), lambda i: (i, 0)),
        scratch_shapes=[pltpu.VMEM((nt, tk, tq), I32), pltpu.VMEM((SPA_LAT, SPA_HEADS * tq), F32)],
        compiler_params=_cparams(("arbitrary",)),
        name="sparse_attn",
    )(iqt, q8t, wt, hb, ikn, cn, cnt, wuv)


def _out_kernel(oa_ref, ob_ref, oc_ref, x_ref, w_ref, lg_ref, lb_ref, y_ref, yb_ref, acc_ref, *, alpha, nj, tn):
    j = pl.program_id(1)
    na, nb = oa_ref.shape[1], ob_ref.shape[1]
    acc_ref[j] = (jnp.dot(oa_ref[...], w_ref[0:na, :], preferred_element_type=F32)
                  + jnp.dot(ob_ref[...], w_ref[na:na + nb, :], preferred_element_type=F32)
                  + jnp.dot(oc_ref[...], w_ref[na + nb:, :], preferred_element_type=F32))

    @pl.when(j == nj - 1)
    def _():
        d = nj * tn
        ssum = 0.0
        for jj in range(nj):
            r = alpha * x_ref[:, jj * tn:(jj + 1) * tn] + acc_ref[jj]
            acc_ref[jj] = r
            ssum = ssum + jnp.sum(r, axis=-1, keepdims=True)
        mu = ssum / d
        vsum = 0.0
        for jj in range(nj):
            c = acc_ref[jj] - mu
            vsum = vsum + jnp.sum(c * c, axis=-1, keepdims=True)
        rstd = lax.rsqrt(vsum / d + LN_EPS)
        for jj in range(nj):
            cs = slice(jj * tn, (jj + 1) * tn)
            y = (acc_ref[jj] - mu) * rstd * lg_ref[:, cs] + lb_ref[:, cs]
            y_ref[:, cs] = y
            yb_ref[:, cs] = y.astype(BF16)


def _out_proj(oa, ob, oc, x, w, lg, lb, alpha, tm, tn):
    s, d = x.shape
    nj = d // tn
    row = lambda i, j: (i, 0)
    return pl.pallas_call(
        functools.partial(_out_kernel, alpha=alpha, nj=nj, tn=tn),
        out_shape=(jax.ShapeDtypeStruct((s, d), F32), jax.ShapeDtypeStruct((s, d), BF16)),
        grid=(s // tm, nj),
        in_specs=[
            pl.BlockSpec((tm, oa.shape[1]), row),
            pl.BlockSpec((tm, ob.shape[1]), row),
            pl.BlockSpec((tm, oc.shape[1]), row),
            pl.BlockSpec((tm, d), row),
            pl.BlockSpec((w.shape[0], tn), lambda i, j: (0, j)),
            pl.BlockSpec((1, d), lambda i, j: (0, 0)),
            pl.BlockSpec((1, d), lambda i, j: (0, 0)),
        ],
        out_specs=(pl.BlockSpec((tm, d), row), pl.BlockSpec((tm, d), row)),
        scratch_shapes=[pltpu.VMEM((nj, tm, tn), F32)],
        compiler_params=_cparams(("parallel", "arbitrary")),
        name="out_proj_ln",
    )(oa, ob, oc, x, w, lg, lb)


def kernel(x, w_in, w_out, gla_w_gate2, gla_b_gate, gla_norm_g, diff_lambda, diff_norm_g,
           spa_kv_norm_g, spa_ik_ln_g, spa_ik_ln_b, spa_w_uv, post_ln_g, post_ln_b):
    bsz, s, d = x.shape
    assert bsz == 1 and d == 4096 and w_in.shape[2] == sum(_IN_WIDTHS)
    depth = w_in.shape[0]
    alpha = (2.0 * depth) ** 0.25
    topk = min(IDX_TOPK_MAX, s // 4)
    tm_in = min(1024, s)
    t_diff = min(512, s)
    tq_s, tk_s = 128, min(512, s)
    nq_s = s // tq_s

    xf = x[0]
    xb = xf.astype(BF16)
    for l in range(depth):
        wb, wf = _prep_w_in(w_in[l])
        hb = _matmul(xb, wb, BF16, tm_in, 512)
        hf = _matmul(xb, wf, F32, tm_in, 512)

        o_a = _gla(hb, hf, gla_w_gate2[l], gla_b_gate[l][None], gla_norm_g[l][None], 128)

        lam_init = 0.8 - 0.6 * math.exp(-0.3 * l)
        o_b = _diff(hb, diff_lambda[l], diff_norm_g[l][None], lam_init, t_diff)

        cn, cnt, ikn = _spa_norm(hf, spa_kv_norm_g[l][None], spa_ik_ln_g[l][None], spa_ik_ln_b[l][None], tk_s)
        ciq = lax.slice_in_dim(hb, _HB_OFF["ciq"], _HB_OFF["ciq"] + 1024, axis=1)
        iqt = ciq.reshape(nq_s, tq_s, IDX_HEADS, IDX_DIM).transpose(0, 3, 2, 1).reshape(nq_s, IDX_DIM, IDX_HEADS * tq_s)
        cq = lax.slice_in_dim(hb, _HB_OFF["cq"], _HB_OFF["cq"] + 2048, axis=1)
        q8t = cq.reshape(nq_s, tq_s, SPA_HEADS, SPA_LAT).transpose(0, 3, 2, 1).reshape(nq_s, SPA_LAT, SPA_HEADS * tq_s)
        ciw = lax.slice_in_dim(hf, HF_SMALL + SM_CIW, HF_SMALL + SM_CIW + IDX_HEADS, axis=1)
        wt = (ciw * (IDX_HEADS ** -0.5 * IDX_DIM ** -0.5)).reshape(nq_s, tq_s, IDX_HEADS).transpose(0, 2, 1)
        o_c = _spa(hb, iqt, q8t, wt, ikn, cn, cnt, spa_w_uv[l].astype(BF16), tq_s, tk_s, topk)

        xf, xb = _out_proj(o_a, o_b, o_c, xf, w_out[l].astype(BF16), post_ln_g[l][None], post_ln_b[l][None],
                           alpha, 256, 1024)
    return xf[None]
```

```python
import functools
import math

import jax
import jax.numpy as jnp
from jax import lax
from jax.experimental import pallas as pl
from jax.experimental.pallas import tpu as pltpu

F32 = jnp.float32
BF16 = jnp.bfloat16
I32 = jnp.int32

CHUNK = 64
GLA_HEADS, GLA_DK, GLA_DV, GLA_RANK, GLA_TAU = 4, 256, 512, 16, 16.0
DIFF_HEADS, DIFF_DQK, DIFF_DV = 4, 128, 256
SPA_HEADS, SPA_DV, SPA_LAT = 8, 128, 256
IDX_HEADS, IDX_DIM, IDX_TOPK_MAX = 16, 64, 256
LN_EPS, RMS_EPS = 1e-5, 1e-6

_IN_WIDTHS = (1024, 1024, 2048, 16, 2048, 1024, 1024, 1024, 1024, 2048, 256, 1024, 64, 16, 1024)
_IN_NAMES = ("aq", "ak", "av", "aa", "ag", "bq", "bk", "bv", "bg", "cq", "ckv", "ciq", "cik", "ciw", "cg")
_IN_OFF = {}
_acc = 0
for _n, _w in zip(_IN_NAMES, _IN_WIDTHS):
    _IN_OFF[_n] = (_acc, _w)
    _acc += _w

_HB_ORDER = ("av", "ag", "aq", "bq", "bk", "bv", "bg", "cg")
_HB_OFF = {}
_acc = 0
for _n in _HB_ORDER:
    _HB_OFF[_n] = _acc
    _acc += _IN_OFF[_n][1]
HB_WIDTH = _acc
HT_CQ, HT_CIQ, HT_WIDTH = 0, 2048, 3072
HF_AK, HF_CKV, HF_SMALL, HF_WIDTH = 0, 1024, 1280, 1536
SM_CIK, SM_AA, SM_CIW = 0, 64, 80

DIFF_RC = 128
OUT_TK = 512
OUT_NC = 1024
VMEM_LIMIT = 56 * 1024 * 1024
NEG_BIG = -1e30
INT_MIN = -(2 ** 31)


def _cparams(sem):
    return pltpu.CompilerParams(dimension_semantics=sem, vmem_limit_bytes=VMEM_LIMIT)


def _silu(g):
    return g / (1.0 + jnp.exp(-g))


def _matmul_kernel(x_ref, w_ref, o_ref):
    o_ref[...] = jnp.dot(x_ref[...], w_ref[...], preferred_element_type=F32).astype(o_ref.dtype)


def _matmul(x, w, out_dtype, tm, tn):
    m, k = x.shape
    n = w.shape[1]
    return pl.pallas_call(
        _matmul_kernel,
        out_shape=jax.ShapeDtypeStruct((m, n), out_dtype),
        grid=(m // tm, n // tn),
        in_specs=[pl.BlockSpec((tm, k), lambda i, j: (i, 0)),
                  pl.BlockSpec((k, tn), lambda i, j: (0, j))],
        out_specs=pl.BlockSpec((tm, tn), lambda i, j: (i, j)),
        compiler_params=_cparams(("parallel", "parallel")),
        name="in_proj",
    )(x, w)


def _matmul_t_kernel(wt_ref, x_ref, o_ref):
    o_ref[...] = lax.dot_general(wt_ref[...], x_ref[...], (((1,), (1,)), ((), ())),
                                 preferred_element_type=F32).astype(o_ref.dtype)


def _matmul_t(wt, x, out_dtype, tn, tm):
    n, k = wt.shape
    m = x.shape[0]
    return pl.pallas_call(
        _matmul_t_kernel,
        out_shape=jax.ShapeDtypeStruct((n, m), out_dtype),
        grid=(m // tm, n // tn),
        in_specs=[pl.BlockSpec((tn, k), lambda i, j: (j, 0)),
                  pl.BlockSpec((tm, k), lambda i, j: (i, 0))],
        out_specs=pl.BlockSpec((tn, tm), lambda i, j: (j, i)),
        compiler_params=_cparams(("parallel", "parallel")),
        name="in_proj_t",
    )(wt, x)


def _prep_w_in(w):
    def cols(name):
        o, n = _IN_OFF[name]
        return w[:, o:o + n]
    d = w.shape[0]
    wb = jnp.concatenate([cols(n) for n in _HB_ORDER], axis=1).astype(BF16)
    small = jnp.concatenate([cols("cik"), cols("aa"), cols("ciw"), jnp.zeros((d, 32), w.dtype)], axis=1)
    wf = jnp.concatenate([cols("ak"), cols("ckv"), small, jnp.zeros((d, HF_WIDTH - HF_SMALL - 128), w.dtype)],
                         axis=1).astype(BF16)
    wt = jnp.concatenate([cols("cq"), cols("ciq")], axis=1).astype(BF16).T
    return wb, wf, wt


def _gla_kernel(v_ref, q_ref, g_ref, k_ref, sm_ref, w2_ref, b_ref, ng_ref, o_ref, st_ref, *, nchunk):
    @pl.when(pl.program_id(0) == 0)
    def _():
        st_ref[...] = jnp.zeros_like(st_ref)

    row = lax.broadcasted_iota(I32, (CHUNK, CHUNK), 0)
    col = lax.broadcasted_iota(I32, (CHUNK, CHUNK), 1)
    tri = (col <= row).astype(F32)
    for c in range(nchunk):
        rs = slice(c * CHUNK, (c + 1) * CHUNK)
        aa = sm_ref[rs, SM_AA:SM_AA + GLA_RANK].astype(BF16)
        for h in range(GLA_HEADS):
            ks = slice(h * GLA_DK, (h + 1) * GLA_DK)
            vs = slice(h * GLA_DV, (h + 1) * GLA_DV)
            z = jnp.dot(aa, w2_ref[:, ks].astype(BF16), preferred_element_type=F32) + b_ref[:, ks]
            log_a = (jnp.minimum(z, 0.0) - jnp.log1p(jnp.exp(-jnp.abs(z)))) * (1.0 / GLA_TAU)
            cum = jnp.dot(tri, log_a, preferred_element_type=F32, precision=lax.Precision.HIGHEST)
            tot = cum[CHUNK - 1:CHUNK, :]
            k_dec = (k_ref[rs, ks] * jnp.exp(tot - cum)).astype(BF16)
            upd = lax.dot_general(v_ref[rs, vs], k_dec, (((0,), (0,)), ((), ())), preferred_element_type=F32)
            st = st_ref[h] * jnp.exp(tot) + upd
            st_ref[h] = st
            o = lax.dot_general(q_ref[rs, ks], st.astype(BF16), (((1,), (1,)), ((), ())),
                                preferred_element_type=F32) * (GLA_DK ** -0.5)
            var = jnp.mean(o * o, axis=-1, keepdims=True)
            y = o * lax.rsqrt(var + RMS_EPS) * ng_ref[...]
            g = g_ref[rs, vs].astype(F32)
            o_ref[rs, vs] = (y * _silu(g)).astype(o_ref.dtype)


def _gla(hb, hf, w2, b, ng, tr):
    s = hb.shape[0]
    nchunk = tr // CHUNK
    return pl.pallas_call(
        functools.partial(_gla_kernel, nchunk=nchunk),
        out_shape=jax.ShapeDtypeStruct((s, GLA_HEADS * GLA_DV), BF16),
        grid=(s // tr,),
        in_specs=[
            pl.BlockSpec((tr, 2048), lambda i: (i, _HB_OFF["av"] // 2048)),
            pl.BlockSpec((tr, 1024), lambda i: (i, _HB_OFF["aq"] // 1024)),
            pl.BlockSpec((tr, 2048), lambda i: (i, _HB_OFF["ag"] // 2048)),
            pl.BlockSpec((tr, 1024), lambda i: (i, HF_AK // 1024)),
            pl.BlockSpec((tr, 128), lambda i: (i, HF_SMALL // 128)),
            pl.BlockSpec((GLA_RANK, 1024), lambda i: (0, 0)),
            pl.BlockSpec((1, 1024), lambda i: (0, 0)),
            pl.BlockSpec((1, GLA_DV), lambda i: (0, 0)),
        ],
        out_specs=pl.BlockSpec((tr, 2048), lambda i: (i, 0)),
        scratch_shapes=[pltpu.VMEM((GLA_HEADS, GLA_DV, GLA_DK), F32)],
        compiler_params=_cparams(("arbitrary",)),
        name="gla",
    )(hb, hb, hb, hf, hf, w2, b, ng)


def _diff_kernel(lam_ref, q_ref, k_ref, v_ref, g_ref, ng_ref, o_ref, acc_ref, *, t, lam_init):
    i = pl.program_id(1)
    scale = DIFF_DQK ** -0.5
    lp = lam_ref[...]
    lam = (jnp.exp(jnp.sum(lp[0:1] * lp[1:2], axis=(0, 1), keepdims=True))
           - jnp.exp(jnp.sum(lp[2:3] * lp[3:4], axis=(0, 1), keepdims=True)) + lam_init)
    acc_ref[...] = jnp.zeros_like(acc_ref)
    c2 = scale * math.log2(math.e)
    rc = DIFF_RC
    nrc = t // rc

    def tile(j, carry, masked):
        off = pl.multiple_of(j * t, t)
        kt = k_ref[pl.ds(off, t), :]
        vt = v_ref[pl.ds(off, t), :]
        out = []
        for mp in range(2):
            ktm = kt[:, mp * DIFF_DQK:(mp + 1) * DIFF_DQK]
            for r in range(nrc):
                rs = slice(r * rc, (r + 1) * rc)
                m_old, l_old = carry[mp * nrc + r]
                s = lax.dot_general(q_ref[rs, mp * DIFF_DQK:(mp + 1) * DIFF_DQK], ktm, (((1,), (1,)), ((), ())),
                                    preferred_element_type=F32)
                if masked:
                    rch = (lax.broadcasted_iota(I32, (rc, t), 0) + r * rc) // CHUNK
                    cch = lax.broadcasted_iota(I32, (rc, t), 1) // CHUNK
                    s = jnp.where(cch <= rch, s, NEG_BIG)
                m_new = jnp.maximum(m_old, jnp.max(s, axis=-1, keepdims=True))
                p = jnp.exp2((s - m_new) * c2)
                alpha = jnp.exp2((m_old - m_new) * c2)
                l_new = alpha * l_old + jnp.sum(p, axis=-1, keepdims=True)
                acc_ref[mp, rs, :] = acc_ref[mp, rs, :] * alpha + jnp.dot(p.astype(BF16), vt,
                                                                         preferred_element_type=F32)
                out.append((m_new, l_new))
        return tuple(out)

    init = tuple((jnp.full((rc, 1), NEG_BIG, F32), jnp.zeros((rc, 1), F32)) for _ in range(2 * nrc))
    carry = lax.fori_loop(0, i, lambda j, c: tile(j, c, False), init)
    fin = tile(i, carry, True)
    l0 = jnp.concatenate([fin[r][1] for r in range(nrc)], axis=0)
    l1 = jnp.concatenate([fin[nrc + r][1] for r in range(nrc)], axis=0)
    o = acc_ref[0] / l0 - lam * (acc_ref[1] / l1)
    var = jnp.mean(o * o, axis=-1, keepdims=True)
    y = o * lax.rsqrt(var + RMS_EPS) * ng_ref[...] * (1.0 - lam_init)
    g = g_ref[...].astype(F32)
    o_ref[...] = (y * _silu(g)).astype(o_ref.dtype)


def _diff(hb, lam_p, ng, lam_init, t):
    s = hb.shape[0]
    w = DIFF_DV
    return pl.pallas_call(
        functools.partial(_diff_kernel, t=t, lam_init=lam_init),
        out_shape=jax.ShapeDtypeStruct((s, DIFF_HEADS * DIFF_DV), BF16),
        grid=(DIFF_HEADS, s // t),
        in_specs=[
            pl.BlockSpec((4, DIFF_DQK), lambda h, i: (0, 0)),
            pl.BlockSpec((t, w), lambda h, i: (i, _HB_OFF["bq"] // w + h)),
            pl.BlockSpec((s, w), lambda h, i: (0, _HB_OFF["bk"] // w + h)),
            pl.BlockSpec((s, w), lambda h, i: (0, _HB_OFF["bv"] // w + h)),
            pl.BlockSpec((t, w), lambda h, i: (i, _HB_OFF["bg"] // w + h)),
            pl.BlockSpec((1, DIFF_DV), lambda h, i: (0, 0)),
        ],
        out_specs=pl.BlockSpec((t, w), lambda h, i: (i, h)),
        scratch_shapes=[pltpu.VMEM((2, t, DIFF_DV), F32)],
        compiler_params=_cparams(("parallel", "arbitrary")),
        name="diff_attn",
    )(lam_p, hb, hb, hb, hb, ng)


def _spa_norm_kernel(ckv_ref, sm_ref, kvg_ref, lg_ref, lb_ref, cn_ref, cnt_ref, ikn_ref):
    c = ckv_ref[...]
    cn = c * lax.rsqrt(jnp.mean(c * c, axis=-1, keepdims=True) + RMS_EPS) * kvg_ref[...]
    cn_ref[...] = cn.astype(cn_ref.dtype)
    cnt_ref[...] = cn.T.astype(cnt_ref.dtype)
    ik = sm_ref[:, SM_CIK:SM_CIK + IDX_DIM]
    mu = jnp.mean(ik, axis=-1, keepdims=True)
    d = ik - mu
    var = jnp.mean(d * d, axis=-1, keepdims=True)
    ikn_ref[...] = (d * lax.rsqrt(var + LN_EPS) * lg_ref[...] + lb_ref[...]).astype(ikn_ref.dtype)


def _spa_norm(hf, kvg, lg, lb, tk):
    s = hf.shape[0]
    nt = s // tk
    return pl.pallas_call(
        _spa_norm_kernel,
        out_shape=(jax.ShapeDtypeStruct((nt, tk, SPA_LAT), BF16),
                   jax.ShapeDtypeStruct((nt, SPA_LAT, tk), BF16),
                   jax.ShapeDtypeStruct((nt, tk, IDX_DIM), BF16)),
        grid=(nt,),
        in_specs=[
            pl.BlockSpec((tk, SPA_LAT), lambda i: (i, HF_CKV // SPA_LAT)),
            pl.BlockSpec((tk, 128), lambda i: (i, HF_SMALL // 128)),
            pl.BlockSpec((1, SPA_LAT), lambda i: (0, 0)),
            pl.BlockSpec((1, IDX_DIM), lambda i: (0, 0)),
            pl.BlockSpec((1, IDX_DIM), lambda i: (0, 0)),
        ],
        out_specs=(pl.BlockSpec((None, tk, SPA_LAT), lambda i: (i, 0, 0)),
                   pl.BlockSpec((None, SPA_LAT, tk), lambda i: (i, 0, 0)),
                   pl.BlockSpec((None, tk, IDX_DIM), lambda i: (i, 0, 0))),
        compiler_params=_cparams(("parallel",)),
        name="spa_norm",
    )(hf, hf, kvg, lg, lb)


def _spa_kernel(iq_ref, q8_ref, w_ref, g_ref, ikn_ref, cn_ref, cnt_ref, wuv_ref, o_ref, keys_ref, acc_ref,
                iqt_ref, q8t_ref, *, tq, tk, topk, idx_bits):
    i = pl.program_id(0)
    for h in range(IDX_HEADS):
        iqt_ref[:, h * tq:(h + 1) * tq] = iq_ref[h * IDX_DIM:(h + 1) * IDX_DIM, :]
    for h in range(SPA_HEADS):
        q8t_ref[:, h * tq:(h + 1) * tq] = q8_ref[h * SPA_LAT:(h + 1) * SPA_LAT, :]
    n_t = (i * tq + tq + tk - 1) // tk
    qpos = i * tq + lax.broadcasted_iota(I32, (1, tq), 1)
    qch = qpos // CHUNK
    k_row = jnp.minimum(topk, (qch + 1) * CHUNK)
    hk = tk // 2

    def score_tile(j, masked):
        for half in range(2):
            ks = slice(half * hk, (half + 1) * hk)
            ikt = ikn_ref[j, ks, :]
            sc = jnp.zeros((hk, tq), F32)
            for hp in range(IDX_HEADS // 2):
                r = jnp.dot(ikt, iqt_ref[:, hp * 2 * tq:(hp + 1) * 2 * tq], preferred_element_type=F32)
                for hh in range(2):
                    h = hp * 2 + hh
                    sc = sc + jnp.maximum(r[:, hh * tq:(hh + 1) * tq], 0.0) * w_ref[h:h + 1, :]
            bits = pltpu.bitcast(sc, I32)
            key = bits ^ ((bits >> 31) & 0x7FFFFFFF)
            if masked:
                kch = (j * tk + half * hk + lax.broadcasted_iota(I32, (hk, tq), 0)) // CHUNK
                key = jnp.where(kch <= qch, key, INT_MIN)
            keys_ref[j, ks, :] = key

    def body_a(j, _):
        score_tile(j, False)
        return 0

    lax.fori_loop(0, n_t - 1, body_a, 0)
    score_tile(n_t - 1, True)

    keys_ref[n_t] = jnp.full((tk, tq), INT_MIN, I32)
    n_t2 = (n_t + 1) // 2

    def count_where(pred):
        def body(jj, cnt):
            for j in (2 * jj, 2 * jj + 1):
                ind = jnp.where(pred(keys_ref[j], j), 1, 0).astype(I32)
                cnt = cnt + jnp.sum(ind.reshape(tk // 8, 8, tq), axis=0)
            return cnt
        cnt = lax.fori_loop(0, n_t2, body, jnp.zeros((8, tq), I32))
        return jnp.sum(cnt, axis=0, keepdims=True)

    def bis_cond(st):
        b, _, cnt_a = st
        return jnp.logical_and(b < 32, jnp.max(cnt_a - k_row) > 0)

    def bis_body(st):
        b, a, cnt_a = st
        cand_a = a | lax.shift_left(jnp.int32(1), 31 - b)
        cand = cand_a ^ INT_MIN
        cnt = count_where(lambda kt, j: kt >= cand)
        take = cnt >= k_row
        return b + 1, jnp.where(take, cand_a, a), jnp.where(take, cnt, cnt_a)

    _, a_fin, cnt_fin = lax.while_loop(
        bis_cond, bis_body, (jnp.int32(0), jnp.zeros((1, tq), I32), jnp.zeros((1, tq), I32) + n_t * tk))
    thr = a_fin ^ INT_MIN

    @pl.when(jnp.max(cnt_fin - k_row) > 0)
    def _():
        need = k_row - count_where(lambda kt, j: kt > thr)

        def idx_of(j):
            return j * tk + lax.broadcasted_iota(I32, (tk, tq), 0)

        def idx_body(b, y):
            cand = y | lax.shift_left(jnp.int32(1), idx_bits - 1 - b)
            cnt = count_where(lambda kt, j: jnp.logical_and(kt == thr, idx_of(j) < cand))
            return jnp.where(cnt < need, cand, y)

        y_keep = lax.fori_loop(0, idx_bits, idx_body, jnp.zeros((1, tq), I32))

        def fix_body(j, _):
            kt = keys_ref[j]
            drop = jnp.logical_and(kt == thr, idx_of(j) > y_keep)
            keys_ref[j] = jnp.where(drop, kt - 1, kt)
            return 0

        lax.fori_loop(0, n_t, fix_body, 0)

    acc_ref[...] = jnp.zeros_like(acc_ref)
    c2 = SPA_LAT ** -0.5 * math.log2(math.e)

    def body_c(j, carry):
        m_old, l_old = carry
        st = jnp.dot(cn_ref[j], q8t_ref[...], preferred_element_type=F32)
        madd = jnp.where(keys_ref[j] >= thr, 0.0, NEG_BIG)
        s = st + jnp.concatenate([madd] * SPA_HEADS, axis=1)
        m_new = jnp.maximum(m_old, jnp.max(s, axis=0, keepdims=True))
        p = jnp.exp2((s - m_new) * c2)
        alpha = jnp.exp2((m_old - m_new) * c2)
        l_new = alpha * l_old + jnp.sum(p, axis=0, keepdims=True)
        acc_ref[...] = acc_ref[...] * alpha + jnp.dot(cnt_ref[j], p.astype(BF16), preferred_element_type=F32)
        return m_new, l_new

    init = (jnp.full((1, SPA_HEADS * tq), NEG_BIG, F32), jnp.zeros((1, SPA_HEADS * tq), F32))
    _, l_fin = lax.fori_loop(0, n_t, body_c, init)

    for h in range(SPA_HEADS):
        cs = slice(h * tq, (h + 1) * tq)
        o_lat = (acc_ref[:, cs] / l_fin[:, cs]).T
        oh = jnp.dot(o_lat.astype(BF16), wuv_ref[h], preferred_element_type=F32)
        g = g_ref[:, h * SPA_DV:(h + 1) * SPA_DV].astype(F32)
        o_ref[:, h * SPA_DV:(h + 1) * SPA_DV] = (oh * _silu(g)).astype(o_ref.dtype)


def _spa(hb, ht, wt, ikn, cn, cnt, wuv, tq, tk, topk):
    s = hb.shape[0]
    nq, nt = s // tq, s // tk
    const3 = lambda i: (0, 0, 0)
    return pl.pallas_call(
        functools.partial(_spa_kernel, tq=tq, tk=tk, topk=topk, idx_bits=(s - 1).bit_length()),
        out_shape=jax.ShapeDtypeStruct((s, SPA_HEADS * SPA_DV), BF16),
        grid=(nq,),
        in_specs=[
            pl.BlockSpec((IDX_HEADS * IDX_DIM, tq), lambda i: (HT_CIQ // (IDX_HEADS * IDX_DIM), i)),
            pl.BlockSpec((SPA_HEADS * SPA_LAT, tq), lambda i: (HT_CQ // (SPA_HEADS * SPA_LAT), i)),
            pl.BlockSpec((None, IDX_HEADS, tq), lambda i: (i, 0, 0)),
            pl.BlockSpec((tq, 1024), lambda i: (i, _HB_OFF["cg"] // 1024)),
            pl.BlockSpec((nt, tk, IDX_DIM), const3, pipeline_mode=pl.Buffered(1)),
            pl.BlockSpec((nt, tk, SPA_LAT), const3, pipeline_mode=pl.Buffered(1)),
            pl.BlockSpec((nt, SPA_LAT, tk), const3, pipeline_mode=pl.Buffered(1)),
            pl.BlockSpec((SPA_HEADS, SPA_LAT, SPA_DV), const3, pipeline_mode=pl.Buffered(1)),
        ],
        out_specs=pl.BlockSpec((tq, 1024), lambda i: (i, 0)),
        scratch_shapes=[pltpu.VMEM((nt + 1, tk, tq), I32), pltpu.VMEM((SPA_LAT, SPA_HEADS * tq), F32),
                        pltpu.VMEM((IDX_DIM, IDX_HEADS * tq), BF16), pltpu.VMEM((SPA_LAT, SPA_HEADS * tq), BF16)],
        compiler_params=_cparams(("arbitrary",)),
        name="sparse_attn",
    )(ht, ht, wt, hb, ikn, cn, cnt, wuv)


def _out_kernel(oa_ref, ob_ref, oc_ref, x_ref, w_ref, lg_ref, lb_ref, y_ref, yb_ref, *, alpha, ka, kb, nk):
    kk = pl.program_id(1)
    d = y_ref.shape[1]
    ncol = d // OUT_NC

    @pl.when(kk == 0)
    def _():
        y_ref[...] = alpha * x_ref[...]

    def accumulate(o_ref):
        o = o_ref[...]
        for c in range(ncol):
            cs = slice(c * OUT_NC, (c + 1) * OUT_NC)
            y_ref[:, cs] += jnp.dot(o, w_ref[:, cs], preferred_element_type=F32)

    @pl.when(kk < ka)
    def _():
        accumulate(oa_ref)

    @pl.when(jnp.logical_and(kk >= ka, kk < ka + kb))
    def _():
        accumulate(ob_ref)

    @pl.when(kk >= ka + kb)
    def _():
        accumulate(oc_ref)

    @pl.when(kk == nk - 1)
    def _():
        ssum = 0.0
        for c in range(ncol):
            ssum = ssum + jnp.sum(y_ref[:, c * OUT_NC:(c + 1) * OUT_NC], axis=-1, keepdims=True)
        mu = ssum / d
        vsum = 0.0
        for c in range(ncol):
            cen = y_ref[:, c * OUT_NC:(c + 1) * OUT_NC] - mu
            vsum = vsum + jnp.sum(cen * cen, axis=-1, keepdims=True)
        rstd = lax.rsqrt(vsum / d + LN_EPS)
        for c in range(ncol):
            cs = slice(c * OUT_NC, (c + 1) * OUT_NC)
            y = (y_ref[:, cs] - mu) * rstd * lg_ref[:, cs] + lb_ref[:, cs]
            y_ref[:, cs] = y
            yb_ref[:, cs] = y.astype(BF16)


def _out_proj(oa, ob, oc, x, w, lg, lb, alpha, tm):
    s, d = x.shape
    ka, kb, kc = oa.shape[1] // OUT_TK, ob.shape[1] // OUT_TK, oc.shape[1] // OUT_TK
    nk = ka + kb + kc
    row = lambda i, k: (i, 0)
    return pl.pallas_call(
        functools.partial(_out_kernel, alpha=alpha, ka=ka, kb=kb, nk=nk),
        out_shape=(jax.ShapeDtypeStruct((s, d), F32), jax.ShapeDtypeStruct((s, d), BF16)),
        grid=(s // tm, nk),
        in_specs=[
            pl.BlockSpec((tm, OUT_TK), lambda i, k: (i, jnp.minimum(k, ka - 1))),
            pl.BlockSpec((tm, OUT_TK), lambda i, k: (i, jnp.clip(k - ka, 0, kb - 1))),
            pl.BlockSpec((tm, OUT_TK), lambda i, k: (i, jnp.clip(k - ka - kb, 0, kc - 1))),
            pl.BlockSpec((tm, d), row, pipeline_mode=pl.Buffered(1)),
            pl.BlockSpec((OUT_TK, d), lambda i, k: (k, 0)),
            pl.BlockSpec((1, d), lambda i, k: (0, 0)),
            pl.BlockSpec((1, d), lambda i, k: (0, 0)),
        ],
        out_specs=(pl.BlockSpec((tm, d), row), pl.BlockSpec((tm, d), row)),
        compiler_params=_cparams(("parallel", "arbitrary")),
        name="out_proj_ln",
    )(oa, ob, oc, x, w, lg, lb)


def kernel(x, w_in, w_out, gla_w_gate2, gla_b_gate, gla_norm_g, diff_lambda, diff_norm_g,
           spa_kv_norm_g, spa_ik_ln_g, spa_ik_ln_b, spa_w_uv, post_ln_g, post_ln_b):
    bsz, s, d = x.shape
    assert bsz == 1 and d == 4096 and w_in.shape[2] == sum(_IN_WIDTHS)
    depth = w_in.shape[0]
    alpha = (2.0 * depth) ** 0.25
    topk = min(IDX_TOPK_MAX, s // 4)
    tm_in = min(1024, s)
    t_diff = min(512, s)
    tq_s, tk_s = 128, min(512, s)
    nq_s = s // tq_s

    xf = x[0]
    xb = xf.astype(BF16)
    for l in range(depth):
        wb, wf, wqt = _prep_w_in(w_in[l])
        hb = _matmul(xb, wb, BF16, tm_in, 512)
        hf = _matmul(xb, wf, F32, tm_in, 512)
        ht = _matmul_t(wqt, xb, BF16, 512, tm_in)

        o_a = _gla(hb, hf, gla_w_gate2[l], gla_b_gate[l][None], gla_norm_g[l][None], 128)

        lam_init = 0.8 - 0.6 * math.exp(-0.3 * l)
        o_b = _diff(hb, diff_lambda[l], diff_norm_g[l][None], lam_init, t_diff)

        cn, cnt, ikn = _spa_norm(hf, spa_kv_norm_g[l][None], spa_ik_ln_g[l][None], spa_ik_ln_b[l][None], tk_s)
        ciw = lax.slice_in_dim(hf, HF_SMALL + SM_CIW, HF_SMALL + SM_CIW + IDX_HEADS, axis=1)
        wt = (ciw * (IDX_HEADS ** -0.5 * IDX_DIM ** -0.5)).reshape(nq_s, tq_s, IDX_HEADS).transpose(0, 2, 1)
        o_c = _spa(hb, ht, wt, ikn, cn, cnt, spa_w_uv[l].astype(BF16), tq_s, tk_s, topk)

        xf, xb = _out_proj(o_a, o_b, o_c, xf, w_out[l].astype(BF16), post_ln_g[l][None], post_ln_b[l][None],
                           alpha, min(512, s))
    return xf[None]
```

```python
import functools
import math

import jax
import jax.numpy as jnp
from jax import lax
from jax.experimental import pallas as pl
from jax.experimental.pallas import tpu as pltpu

F32 = jnp.float32
BF16 = jnp.bfloat16
I32 = jnp.int32

CHUNK = 64
GLA_HEADS, GLA_DK, GLA_DV, GLA_RANK, GLA_TAU = 4, 256, 512, 16, 16.0
DIFF_HEADS, DIFF_DQK, DIFF_DV = 4, 128, 256
SPA_HEADS, SPA_DV, SPA_LAT = 8, 128, 256
IDX_HEADS, IDX_DIM, IDX_TOPK_MAX = 16, 64, 256
SPA_LATP = SPA_LAT + 16
LN_EPS, RMS_EPS = 1e-5, 1e-6

_IN_WIDTHS = (1024, 1024, 2048, 16, 2048, 1024, 1024, 1024, 1024, 2048, 256, 1024, 64, 16, 1024)
_IN_NAMES = ("aq", "ak", "av", "aa", "ag", "bq", "bk", "bv", "bg", "cq", "ckv", "ciq", "cik", "ciw", "cg")
_IN_OFF = {}
_acc = 0
for _n, _w in zip(_IN_NAMES, _IN_WIDTHS):
    _IN_OFF[_n] = (_acc, _w)
    _acc += _w

_HB_ORDER = ("av", "ag", "aq", "bq", "bk", "bv", "bg", "cg")
_HB_OFF = {}
_acc = 0
for _n in _HB_ORDER:
    _HB_OFF[_n] = _acc
    _acc += _IN_OFF[_n][1]
HB_WIDTH = _acc
HT_CQ, HT_CIQ, HT_WIDTH = 0, 2048, 3072
HF_AK, HF_CKV, HF_SMALL, HF_WIDTH = 0, 1024, 1280, 1536
SM_CIK, SM_AA, SM_CIW = 0, 64, 80

_Q_FOLD = {"bq": DIFF_DQK ** -0.5 * math.log2(math.e), "cq": SPA_LAT ** -0.5 * math.log2(math.e)}
DIFF_RC = 512
DIFF_LAG = 1
OUT_NC = 1024
VMEM_LIMIT = 56 * 1024 * 1024
NEG_BIG = -1e30
INT_MIN = -(2 ** 31)


def _cparams(sem):
    return pltpu.CompilerParams(dimension_semantics=sem, vmem_limit_bytes=VMEM_LIMIT)


def _silu(g):
    return g / (1.0 + jnp.exp(-g))


def _matmul_kernel(x_ref, w_ref, o_ref):
    o_ref[...] = jnp.dot(x_ref[...], w_ref[...], preferred_element_type=F32).astype(o_ref.dtype)


def _matmul(x, w, out_dtype, tm, tn):
    m, k = x.shape
    n = w.shape[1]
    return pl.pallas_call(
        _matmul_kernel,
        out_shape=jax.ShapeDtypeStruct((m, n), out_dtype),
        grid=(m // tm, n // tn),
        in_specs=[pl.BlockSpec((tm, k), lambda i, j: (i, 0)),
                  pl.BlockSpec((k, tn), lambda i, j: (0, j))],
        out_specs=pl.BlockSpec((tm, tn), lambda i, j: (i, j)),
        compiler_params=_cparams(("parallel", "parallel")),
        name="in_proj",
    )(x, w)


def _matmul_t_kernel(wt_ref, x_ref, o_ref):
    o_ref[...] = lax.dot_general(wt_ref[...], x_ref[...], (((1,), (1,)), ((), ())),
                                 preferred_element_type=F32).astype(o_ref.dtype)


def _matmul_t(wt, x, out_dtype, tn, tm):
    n, k = wt.shape
    m = x.shape[0]
    return pl.pallas_call(
        _matmul_t_kernel,
        out_shape=jax.ShapeDtypeStruct((n, m), out_dtype),
        grid=(m // tm, n // tn),
        in_specs=[pl.BlockSpec((tn, k), lambda i, j: (j, 0)),
                  pl.BlockSpec((tm, k), lambda i, j: (i, 0))],
        out_specs=pl.BlockSpec((tn, tm), lambda i, j: (j, i)),
        compiler_params=_cparams(("parallel", "parallel")),
        name="in_proj_t",
    )(wt, x)


def _prep_w_in(w):
    def cols(name):
        o, n = _IN_OFF[name]
        return w[:, o:o + n] * _Q_FOLD.get(name, 1.0)
    d = w.shape[0]
    wb = jnp.concatenate([cols(n) for n in _HB_ORDER], axis=1).astype(BF16)
    small = jnp.concatenate([cols("cik"), cols("aa"), cols("ciw"), jnp.zeros((d, 32), w.dtype)], axis=1)
    wf = jnp.concatenate([cols("ak"), cols("ckv"), small, jnp.zeros((d, HF_WIDTH - HF_SMALL - 128), w.dtype)],
                         axis=1).astype(BF16)
    wt = jnp.concatenate([cols("cq"), cols("ciq")], axis=1).astype(BF16).T
    return wb, wf, wt


def _gla_kernel(v_ref, q_ref, g_ref, k_ref, sm_ref, w2_ref, b_ref, ng_ref, o_ref, st_ref, *, nchunk):
    @pl.when(pl.program_id(0) == 0)
    def _():
        st_ref[...] = jnp.zeros_like(st_ref)

    row = lax.broadcasted_iota(I32, (CHUNK, CHUNK), 0)
    col = lax.broadcasted_iota(I32, (CHUNK, CHUNK), 1)
    tri = (col <= row).astype(F32)
    chunks = [slice(c * CHUNK, (c + 1) * CHUNK) for c in range(nchunk)]
    kss = [slice(h * GLA_DK, (h + 1) * GLA_DK) for h in range(GLA_HEADS)]
    vss = [slice(h * GLA_DV, (h + 1) * GLA_DV) for h in range(GLA_HEADS)]

    aa = sm_ref[:, SM_AA:SM_AA + GLA_RANK].astype(BF16)
    z = jnp.dot(aa, w2_ref[...].astype(BF16), preferred_element_type=F32) + b_ref[...]
    log_a = (jnp.minimum(z, 0.0) - jnp.log1p(jnp.exp(-jnp.abs(z)))) * (1.0 / GLA_TAU)
    cums = [jnp.dot(tri, log_a[rs, :], preferred_element_type=F32, precision=lax.Precision.HIGHEST)
            for rs in chunks]
    tots = [cum[CHUNK - 1:CHUNK, :] for cum in cums]
    k_decs = [(k_ref[rs, :] * jnp.exp(tot - cum)).astype(BF16) for rs, cum, tot in zip(chunks, cums, tots)]
    decays = [jnp.exp(tot) for tot in tots]
    upds = [[lax.dot_general(v_ref[rs, vss[h]], k_dec[:, kss[h]], (((0,), (0,)), ((), ())),
                             preferred_element_type=F32) for h in range(GLA_HEADS)]
            for rs, k_dec in zip(chunks, k_decs)]
    states = [[None] * GLA_HEADS for _ in range(nchunk)]
    for h in range(GLA_HEADS):
        st = st_ref[h]
        for c in range(nchunk):
            st = st * decays[c][:, kss[h]] + upds[c][h]
            states[c][h] = st.astype(BF16)
        st_ref[h] = st
    outs = [[lax.dot_general(q_ref[rs, kss[h]], states[c][h], (((1,), (1,)), ((), ())),
                             preferred_element_type=F32) for h in range(GLA_HEADS)]
            for c, rs in enumerate(chunks)]
    for c, rs in enumerate(chunks):
        for h in range(GLA_HEADS):
            o = outs[c][h] * (GLA_DK ** -0.5)
            var = jnp.mean(o * o, axis=-1, keepdims=True)
            y = o * lax.rsqrt(var + RMS_EPS) * ng_ref[...]
            g = g_ref[rs, vss[h]].astype(F32)
            o_ref[rs, vss[h]] = (y * _silu(g)).astype(o_ref.dtype)


def _gla(hb, hf, w2, b, ng, tr):
    s = hb.shape[0]
    nchunk = tr // CHUNK
    return pl.pallas_call(
        functools.partial(_gla_kernel, nchunk=nchunk),
        out_shape=jax.ShapeDtypeStruct((s, GLA_HEADS * GLA_DV), BF16),
        grid=(s // tr,),
        in_specs=[
            pl.BlockSpec((tr, 2048), lambda i: (i, _HB_OFF["av"] // 2048)),
            pl.BlockSpec((tr, 1024), lambda i: (i, _HB_OFF["aq"] // 1024)),
            pl.BlockSpec((tr, 2048), lambda i: (i, _HB_OFF["ag"] // 2048)),
            pl.BlockSpec((tr, 1024), lambda i: (i, HF_AK // 1024)),
            pl.BlockSpec((tr, 128), lambda i: (i, HF_SMALL // 128)),
            pl.BlockSpec((GLA_RANK, 1024), lambda i: (0, 0)),
            pl.BlockSpec((1, 1024), lambda i: (0, 0)),
            pl.BlockSpec((1, GLA_DV), lambda i: (0, 0)),
        ],
        out_specs=pl.BlockSpec((tr, 2048), lambda i: (i, 0)),
        scratch_shapes=[pltpu.VMEM((GLA_HEADS, GLA_DV, GLA_DK), F32)],
        compiler_params=_cparams(("arbitrary",)),
        name="gla",
    )(hb, hb, hb, hf, hf, w2, b, ng)


def _diff_kernel(lam_ref, q_ref, k_ref, v_ref, g_ref, ng_ref, o_ref, acc_ref, *, t, lam_init):
    i = pl.program_id(1)
    lp = lam_ref[...]
    lam = (jnp.exp(jnp.sum(lp[0:1] * lp[1:2], axis=(0, 1), keepdims=True))
           - jnp.exp(jnp.sum(lp[2:3] * lp[3:4], axis=(0, 1), keepdims=True)) + lam_init)
    acc_ref[...] = jnp.zeros_like(acc_ref)
    rc = DIFF_RC
    nrc = t // rc

    def tile(j, carry, masked):
        off = pl.multiple_of(j * t, t)
        kt = k_ref[pl.ds(off, t), :]
        vt = v_ref[pl.ds(off, t), :]
        chains = [(mp, r) for mp in range(2) for r in range(nrc)]

        def scores(c):
            mp, r = chains[c]
            rs = slice(r * rc, (r + 1) * rc)
            return lax.dot_general(q_ref[rs, mp * DIFF_DQK:(mp + 1) * DIFF_DQK],
                                   kt[:, mp * DIFF_DQK:(mp + 1) * DIFF_DQK], (((1,), (1,)), ((), ())),
                                   preferred_element_type=F32)

        pending = {c: scores(c) for c in range(min(DIFF_LAG, len(chains)))}
        out = []
        for c, (mp, r) in enumerate(chains):
            if c + DIFF_LAG < len(chains):
                pending[c + DIFF_LAG] = scores(c + DIFF_LAG)
            rs = slice(r * rc, (r + 1) * rc)
            m_old, l_old = carry[c]
            s = pending.pop(c)
            if masked:
                rch = (lax.broadcasted_iota(I32, (rc, t), 0) + r * rc) // CHUNK
                cch = lax.broadcasted_iota(I32, (rc, t), 1) // CHUNK
                s = jnp.where(cch <= rch, s, NEG_BIG)
            m_new = jnp.maximum(m_old, jnp.max(s, axis=-1, keepdims=True))
            p = jnp.exp2(s - m_new)
            alpha = jnp.exp2(m_old - m_new)
            l_new = alpha * l_old + jnp.sum(p, axis=-1, keepdims=True)
            acc_ref[mp, rs, :] = acc_ref[mp, rs, :] * alpha + jnp.dot(p.astype(BF16), vt,
                                                                     preferred_element_type=F32)
            out.append((m_new, l_new))
        return tuple(out)

    init = tuple((jnp.full((rc, 1), NEG_BIG, F32), jnp.zeros((rc, 1), F32)) for _ in range(2 * nrc))
    carry = lax.fori_loop(0, i, lambda j, c: tile(j, c, False), init)
    fin = tile(i, carry, True)
    l0 = jnp.concatenate([fin[r][1] for r in range(nrc)], axis=0)
    l1 = jnp.concatenate([fin[nrc + r][1] for r in range(nrc)], axis=0)
    o = acc_ref[0] / l0 - lam * (acc_ref[1] / l1)
    var = jnp.mean(o * o, axis=-1, keepdims=True)
    y = o * lax.rsqrt(var + RMS_EPS) * ng_ref[...] * (1.0 - lam_init)
    g = g_ref[...].astype(F32)
    o_ref[...] = (y * _silu(g)).astype(o_ref.dtype)


def _diff(hb, lam_p, ng, lam_init, t):
    s = hb.shape[0]
    w = DIFF_DV
    return pl.pallas_call(
        functools.partial(_diff_kernel, t=t, lam_init=lam_init),
        out_shape=jax.ShapeDtypeStruct((s, DIFF_HEADS * DIFF_DV), BF16),
        grid=(DIFF_HEADS, s // t),
        in_specs=[
            pl.BlockSpec((4, DIFF_DQK), lambda h, i: (0, 0)),
            pl.BlockSpec((t, w), lambda h, i: (i, _HB_OFF["bq"] // w + h)),
            pl.BlockSpec((s, w), lambda h, i: (0, _HB_OFF["bk"] // w + h)),
            pl.BlockSpec((s, w), lambda h, i: (0, _HB_OFF["bv"] // w + h)),
            pl.BlockSpec((t, w), lambda h, i: (i, _HB_OFF["bg"] // w + h)),
            pl.BlockSpec((1, DIFF_DV), lambda h, i: (0, 0)),
        ],
        out_specs=pl.BlockSpec((t, w), lambda h, i: (i, h)),
        scratch_shapes=[pltpu.VMEM((2, t, DIFF_DV), F32)],
        compiler_params=_cparams(("parallel", "arbitrary")),
        name="diff_attn",
    )(lam_p, hb, hb, hb, hb, ng)


def _spa_norm_kernel(ckv_ref, sm_ref, kvg_ref, lg_ref, lb_ref, cn_ref, cnt_ref, ikn_ref):
    c = ckv_ref[...]
    cn = c * lax.rsqrt(jnp.mean(c * c, axis=-1, keepdims=True) + RMS_EPS) * kvg_ref[...]
    cn_ref[...] = cn.astype(cn_ref.dtype)
    cnt_ref[0:SPA_LAT, :] = cn.T.astype(cnt_ref.dtype)
    pad_rows = cnt_ref.shape[0] - SPA_LAT
    ones_row = lax.broadcasted_iota(I32, (pad_rows, cnt_ref.shape[1]), 0) == 0
    cnt_ref[SPA_LAT:, :] = jnp.where(ones_row, 1.0, 0.0).astype(cnt_ref.dtype)
    ik = sm_ref[:, SM_CIK:SM_CIK + IDX_DIM]
    mu = jnp.mean(ik, axis=-1, keepdims=True)
    d = ik - mu
    var = jnp.mean(d * d, axis=-1, keepdims=True)
    ikn_ref[...] = (d * lax.rsqrt(var + LN_EPS) * lg_ref[...] + lb_ref[...]).astype(ikn_ref.dtype)


def _spa_norm(hf, kvg, lg, lb, tk):
    s = hf.shape[0]
    nt = s // tk
    return pl.pallas_call(
        _spa_norm_kernel,
        out_shape=(jax.ShapeDtypeStruct((nt, tk, SPA_LAT), BF16),
                   jax.ShapeDtypeStruct((nt, SPA_LATP, tk), BF16),
                   jax.ShapeDtypeStruct((nt, tk, IDX_DIM), BF16)),
        grid=(nt,),
        in_specs=[
            pl.BlockSpec((tk, SPA_LAT), lambda i: (i, HF_CKV // SPA_LAT)),
            pl.BlockSpec((tk, 128), lambda i: (i, HF_SMALL // 128)),
            pl.BlockSpec((1, SPA_LAT), lambda i: (0, 0)),
            pl.BlockSpec((1, IDX_DIM), lambda i: (0, 0)),
            pl.BlockSpec((1, IDX_DIM), lambda i: (0, 0)),
        ],
        out_specs=(pl.BlockSpec((None, tk, SPA_LAT), lambda i: (i, 0, 0)),
                   pl.BlockSpec((None, SPA_LATP, tk), lambda i: (i, 0, 0)),
                   pl.BlockSpec((None, tk, IDX_DIM), lambda i: (i, 0, 0))),
        compiler_params=_cparams(("parallel",)),
        name="spa_norm",
    )(hf, hf, kvg, lg, lb)


def _spa_kernel(iq_ref, q8_ref, w_ref, g_ref, ikn_ref, cn_ref, cnt_ref, wuv_ref, o_ref, keys_ref, acc_ref,
                iqt_ref, q8t_ref, *, tq, tk, topk, idx_bits):
    i = pl.program_id(0)
    for h in range(IDX_HEADS):
        iqt_ref[:, h * tq:(h + 1) * tq] = iq_ref[h * IDX_DIM:(h + 1) * IDX_DIM, :]
    for h in range(SPA_HEADS):
        q8t_ref[:, h * tq:(h + 1) * tq] = q8_ref[h * SPA_LAT:(h + 1) * SPA_LAT, :]
    n_t = (i * tq + tq + tk - 1) // tk
    qpos = i * tq + lax.broadcasted_iota(I32, (1, tq), 1)
    qch = qpos // CHUNK
    k_row = jnp.minimum(topk, (qch + 1) * CHUNK)
    hk = tk // 2

    def score_tile(j, masked):
        for half in range(2):
            ks = slice(half * hk, (half + 1) * hk)
            ikt = ikn_ref[j, ks, :]
            sc = jnp.zeros((hk, tq), F32)
            for hp in range(IDX_HEADS // 2):
                r = jnp.dot(ikt, iqt_ref[:, hp * 2 * tq:(hp + 1) * 2 * tq], preferred_element_type=F32)
                for hh in range(2):
                    h = hp * 2 + hh
                    sc = sc + jnp.maximum(r[:, hh * tq:(hh + 1) * tq], 0.0) * w_ref[h:h + 1, :]
            bits = pltpu.bitcast(sc, I32)
            key = bits ^ ((bits >> 31) & 0x7FFFFFFF)
            if masked:
                kch = (j * tk + half * hk + lax.broadcasted_iota(I32, (hk, tq), 0)) // CHUNK
                key = jnp.where(kch <= qch, key, INT_MIN)
            keys_ref[j, ks, :] = key

    def body_a(j, _):
        score_tile(j, False)
        return 0

    lax.fori_loop(0, n_t - 1, body_a, 0)
    score_tile(n_t - 1, True)

    keys_ref[n_t] = jnp.full((tk, tq), INT_MIN, I32)
    n_t2 = (n_t + 1) // 2

    def count_where(pred):
        def body(jj, cnt):
            for j in (2 * jj, 2 * jj + 1):
                ind = jnp.where(pred(keys_ref[j], j), 1, 0).astype(I32)
                cnt = cnt + jnp.sum(ind.reshape(tk // 8, 8, tq), axis=0)
            return cnt
        cnt = lax.fori_loop(0, n_t2, body, jnp.zeros((8, tq), I32))
        return jnp.sum(cnt, axis=0, keepdims=True)

    def bis_cond(st):
        b, _, cnt_a = st
        return jnp.logical_and(b < 32, jnp.max(cnt_a - k_row) > 0)

    def bis_body(st):
        b, a, cnt_a = st
        cand_a = a | lax.shift_left(jnp.int32(1), 31 - b)
        cand = cand_a ^ INT_MIN
        cnt = count_where(lambda kt, j: kt >= cand)
        take = cnt >= k_row
        return b + 1, jnp.where(take, cand_a, a), jnp.where(take, cnt, cnt_a)

    _, a_fin, cnt_fin = lax.while_loop(
        bis_cond, bis_body, (jnp.int32(0), jnp.zeros((1, tq), I32), jnp.zeros((1, tq), I32) + n_t * tk))
    thr = a_fin ^ INT_MIN

    @pl.when(jnp.max(cnt_fin - k_row) > 0)
    def _():
        need = k_row - count_where(lambda kt, j: kt > thr)

        def idx_of(j):
            return j * tk + lax.broadcasted_iota(I32, (tk, tq), 0)

        def idx_body(b, y):
            cand = y | lax.shift_left(jnp.int32(1), idx_bits - 1 - b)
            cnt = count_where(lambda kt, j: jnp.logical_and(kt == thr, idx_of(j) < cand))
            return jnp.where(cnt < need, cand, y)

        y_keep = lax.fori_loop(0, idx_bits, idx_body, jnp.zeros((1, tq), I32))

        def fix_body(j, _):
            kt = keys_ref[j]
            drop = jnp.logical_and(kt == thr, idx_of(j) > y_keep)
            keys_ref[j] = jnp.where(drop, kt - 1, kt)
            return 0

        lax.fori_loop(0, n_t, fix_body, 0)

    acc_ref[...] = jnp.zeros_like(acc_ref)

    def body_c(j, m_old):
        st = jnp.dot(cn_ref[j], q8t_ref[...], preferred_element_type=F32)
        madd = jnp.where(keys_ref[j] >= thr, 0.0, NEG_BIG)
        s = st + jnp.concatenate([madd] * SPA_HEADS, axis=1)
        m_new = jnp.maximum(m_old, jnp.max(s, axis=0, keepdims=True))
        p = jnp.exp2(s - m_new)
        alpha = jnp.exp2(m_old - m_new)
        acc_ref[...] = acc_ref[...] * alpha + jnp.dot(cnt_ref[j], p.astype(BF16), preferred_element_type=F32)
        return m_new

    lax.fori_loop(0, n_t, body_c, jnp.full((1, SPA_HEADS * tq), NEG_BIG, F32))
    l_fin = acc_ref[SPA_LAT:SPA_LAT + 1, :]

    for h in range(SPA_HEADS):
        cs = slice(h * tq, (h + 1) * tq)
        o_lat = (acc_ref[0:SPA_LAT, cs] / l_fin[:, cs]).T
        oh = jnp.dot(o_lat.astype(BF16), wuv_ref[h], preferred_element_type=F32)
        g = g_ref[:, h * SPA_DV:(h + 1) * SPA_DV].astype(F32)
        o_ref[:, h * SPA_DV:(h + 1) * SPA_DV] = (oh * _silu(g)).astype(o_ref.dtype)


def _spa(hb, ht, wt, ikn, cn, cnt, wuv, tq, tk, topk):
    s = hb.shape[0]
    nq, nt = s // tq, s // tk
    const3 = lambda i: (0, 0, 0)
    return pl.pallas_call(
        functools.partial(_spa_kernel, tq=tq, tk=tk, topk=topk, idx_bits=(s - 1).bit_length()),
        out_shape=jax.ShapeDtypeStruct((s, SPA_HEADS * SPA_DV), BF16),
        grid=(nq,),
        in_specs=[
            pl.BlockSpec((IDX_HEADS * IDX_DIM, tq), lambda i: (HT_CIQ // (IDX_HEADS * IDX_DIM), i)),
            pl.BlockSpec((SPA_HEADS * SPA_LAT, tq), lambda i: (HT_CQ // (SPA_HEADS * SPA_LAT), i)),
            pl.BlockSpec((None, IDX_HEADS, tq), lambda i: (i, 0, 0)),
            pl.BlockSpec((tq, 1024), lambda i: (i, _HB_OFF["cg"] // 1024)),
            pl.BlockSpec((nt, tk, IDX_DIM), const3, pipeline_mode=pl.Buffered(1)),
            pl.BlockSpec((nt, tk, SPA_LAT), const3, pipeline_mode=pl.Buffered(1)),
            pl.BlockSpec((nt, SPA_LATP, tk), const3, pipeline_mode=pl.Buffered(1)),
            pl.BlockSpec((SPA_HEADS, SPA_LAT, SPA_DV), const3, pipeline_mode=pl.Buffered(1)),
        ],
        out_specs=pl.BlockSpec((tq, 1024), lambda i: (i, 0)),
        scratch_shapes=[pltpu.VMEM((nt + 1, tk, tq), I32), pltpu.VMEM((SPA_LATP, SPA_HEADS * tq), F32),
                        pltpu.VMEM((IDX_DIM, IDX_HEADS * tq), BF16), pltpu.VMEM((SPA_LAT, SPA_HEADS * tq), BF16)],
        compiler_params=_cparams(("arbitrary",)),
        name="sparse_attn",
    )(ht, ht, wt, hb, ikn, cn, cnt, wuv)


def _out_kernel(oa_ref, ob_ref, oc_ref, x_ref, w_ref, lg_ref, lb_ref, y_ref, yb_ref, *, alpha):
    d = y_ref.shape[1]
    ncol = d // OUT_NC
    na, nb = oa_ref.shape[1], ob_ref.shape[1]
    oa, ob, oc = oa_ref[...], ob_ref[...], oc_ref[...]
    ssum = 0.0
    for c in range(ncol):
        cs = slice(c * OUT_NC, (c + 1) * OUT_NC)
        r = (alpha * x_ref[:, cs]
             + jnp.dot(oa, w_ref[0:na, cs], preferred_element_type=F32)
             + jnp.dot(ob, w_ref[na:na + nb, cs], preferred_element_type=F32)
             + jnp.dot(oc, w_ref[na + nb:, cs], preferred_element_type=F32))
        y_ref[:, cs] = r
        ssum = ssum + jnp.sum(r, axis=-1, keepdims=True)
    mu = ssum / d
    vsum = 0.0
    for c in range(ncol):
        cen = y_ref[:, c * OUT_NC:(c + 1) * OUT_NC] - mu
        vsum = vsum + jnp.sum(cen * cen, axis=-1, keepdims=True)
    rstd = lax.rsqrt(vsum / d + LN_EPS)
    for c in range(ncol):
        cs = slice(c * OUT_NC, (c + 1) * OUT_NC)
        y = (y_ref[:, cs] - mu) * rstd * lg_ref[:, cs] + lb_ref[:, cs]
        y_ref[:, cs] = y
        yb_ref[:, cs] = y.astype(BF16)


def _out_proj(oa, ob, oc, x, w, lg, lb, alpha, tm):
    s, d = x.shape
    row = lambda i: (i, 0)
    const = lambda i: (0, 0)
    return pl.pallas_call(
        functools.partial(_out_kernel, alpha=alpha),
        out_shape=(jax.ShapeDtypeStruct((s, d), F32), jax.ShapeDtypeStruct((s, d), BF16)),
        grid=(s // tm,),
        in_specs=[
            pl.BlockSpec((tm, oa.shape[1]), row),
            pl.BlockSpec((tm, ob.shape[1]), row),
            pl.BlockSpec((tm, oc.shape[1]), row),
            pl.BlockSpec((tm, d), row),
            pl.BlockSpec(w.shape, const, pipeline_mode=pl.Buffered(1)),
            pl.BlockSpec((1, d), const),
            pl.BlockSpec((1, d), const),
        ],
        out_specs=(pl.BlockSpec((tm, d), row), pl.BlockSpec((tm, d), row)),
        compiler_params=_cparams(("parallel",)),
        name="out_proj_ln",
    )(oa, ob, oc, x, w, lg, lb)


def kernel(x, w_in, w_out, gla_w_gate2, gla_b_gate, gla_norm_g, diff_lambda, diff_norm_g,
           spa_kv_norm_g, spa_ik_ln_g, spa_ik_ln_b, spa_w_uv, post_ln_g, post_ln_b):
    bsz, s, d = x.shape
    assert bsz == 1 and d == 4096 and w_in.shape[2] == sum(_IN_WIDTHS)
    depth = w_in.shape[0]
    alpha = (2.0 * depth) ** 0.25
    topk = min(IDX_TOPK_MAX, s // 4)
    tm_in = min(1024, s)
    t_diff = min(512, s)
    tq_s, tk_s = 128, min(512, s)
    nq_s = s // tq_s

    xf = x[0]
    xb = xf.astype(BF16)
    for l in range(depth):
        wb, wf, wqt = _prep_w_in(w_in[l])
        hb = _matmul(xb, wb, BF16, tm_in, 512)
        hf = _matmul(xb, wf, F32, tm_in, 512)
        ht = _matmul_t(wqt, xb, BF16, 512, tm_in)

        o_a = _gla(hb, hf, gla_w_gate2[l], gla_b_gate[l][None], gla_norm_g[l][None], min(256, s))

        lam_init = 0.8 - 0.6 * math.exp(-0.3 * l)
        o_b = _diff(hb, diff_lambda[l], diff_norm_g[l][None], lam_init, t_diff)

        cn, cnt, ikn = _spa_norm(hf, spa_kv_norm_g[l][None], spa_ik_ln_g[l][None], spa_ik_ln_b[l][None], tk_s)
        ciw = lax.slice_in_dim(hf, HF_SMALL + SM_CIW, HF_SMALL + SM_CIW + IDX_HEADS, axis=1)
        wt = (ciw * (IDX_HEADS ** -0.5 * IDX_DIM ** -0.5)).reshape(nq_s, tq_s, IDX_HEADS).transpose(0, 2, 1)
        o_c = _spa(hb, ht, wt, ikn, cn, cnt, spa_w_uv[l].astype(BF16), tq_s, tk_s, topk)

        xf, xb = _out_proj(o_a, o_b, o_c, xf, w_out[l].astype(BF16), post_ln_g[l][None], post_ln_b[l][None],
                           alpha, min(128, s))
    return xf[None]
```

```python
import functools
import math

import jax
import jax.numpy as jnp
from jax import lax
from jax.experimental import pallas as pl
from jax.experimental.pallas import tpu as pltpu

F32 = jnp.float32
BF16 = jnp.bfloat16
I32 = jnp.int32

CHUNK = 64
GLA_HEADS, GLA_DK, GLA_DV, GLA_RANK, GLA_TAU = 4, 256, 512, 16, 16.0
DIFF_HEADS, DIFF_DQK, DIFF_DV = 4, 128, 256
SPA_HEADS, SPA_DV, SPA_LAT = 8, 128, 256
IDX_HEADS, IDX_DIM, IDX_TOPK_MAX = 16, 64, 256
SPA_LATP = SPA_LAT + 16
LN_EPS, RMS_EPS = 1e-5, 1e-6

_IN_WIDTHS = (1024, 1024, 2048, 16, 2048, 1024, 1024, 1024, 1024, 2048, 256, 1024, 64, 16, 1024)
_IN_NAMES = ("aq", "ak", "av", "aa", "ag", "bq", "bk", "bv", "bg", "cq", "ckv", "ciq", "cik", "ciw", "cg")
_IN_OFF = {}
_acc = 0
for _n, _w in zip(_IN_NAMES, _IN_WIDTHS):
    _IN_OFF[_n] = (_acc, _w)
    _acc += _w

_HB_ORDER = ("av", "ag", "aq", "bq", "bk", "bv", "bg", "cg")
_HB_OFF = {}
_acc = 0
for _n in _HB_ORDER:
    _HB_OFF[_n] = _acc
    _acc += _IN_OFF[_n][1]
HB_WIDTH = _acc
HT_CQ, HT_CIQ, HT_WIDTH = 0, 2048, 3072
HF_AK, HF_CKV, HF_SMALL, HF_WIDTH = 0, 1024, 1280, 1536
SM_CIK, SM_AA, SM_CIW = 0, 64, 80

_Q_FOLD = {"bq": DIFF_DQK ** -0.5 * math.log2(math.e), "cq": SPA_LAT ** -0.5 * math.log2(math.e)}
IN_TN = 512
BISECT_BLIND_BITS = 24
OUT_NC = 1024
VMEM_LIMIT = 56 * 1024 * 1024
NEG_BIG = -1e30
INT_MIN = -(2 ** 31)


def _cparams(sem):
    return pltpu.CompilerParams(dimension_semantics=sem, vmem_limit_bytes=VMEM_LIMIT)


def _silu(g):
    return g / (1.0 + jnp.exp(-g))


def _tile_scale(fold):
    if fold is None:
        return 1.0
    lo, hi, c = fold
    j = pl.program_id(1)
    return jnp.where(jnp.logical_and(j >= lo, j < hi), c, 1.0)


def _matmul_kernel(x_ref, w_ref, o_ref, *, fold):
    acc = jnp.dot(x_ref[...], w_ref[...], preferred_element_type=F32)
    o_ref[...] = (acc * _tile_scale(fold)).astype(o_ref.dtype)


def _matmul(x, w, out_dtype, tm, tn, fold=None):
    m, k = x.shape
    n = w.shape[1]
    return pl.pallas_call(
        functools.partial(_matmul_kernel, fold=fold),
        out_shape=jax.ShapeDtypeStruct((m, n), out_dtype),
        grid=(m // tm, n // tn),
        in_specs=[pl.BlockSpec((tm, k), lambda i, j: (i, 0)),
                  pl.BlockSpec((k, tn), lambda i, j: (0, j))],
        out_specs=pl.BlockSpec((tm, tn), lambda i, j: (i, j)),
        compiler_params=_cparams(("parallel", "parallel")),
        name="in_proj",
    )(x, w)


def _matmul_t_kernel(wt_ref, x_ref, o_ref, *, fold):
    acc = lax.dot_general(wt_ref[...], x_ref[...], (((1,), (1,)), ((), ())), preferred_element_type=F32)
    o_ref[...] = (acc * _tile_scale(fold)).astype(o_ref.dtype)


def _matmul_t(wt, x, out_dtype, tn, tm, fold=None):
    n, k = wt.shape
    m = x.shape[0]
    return pl.pallas_call(
        functools.partial(_matmul_t_kernel, fold=fold),
        out_shape=jax.ShapeDtypeStruct((n, m), out_dtype),
        grid=(m // tm, n // tn),
        in_specs=[pl.BlockSpec((tn, k), lambda i, j: (j, 0)),
                  pl.BlockSpec((tm, k), lambda i, j: (i, 0))],
        out_specs=pl.BlockSpec((tn, tm), lambda i, j: (j, i)),
        compiler_params=_cparams(("parallel", "parallel")),
        name="in_proj_t",
    )(wt, x)


def _prep_w_in(w):
    def cols(name):
        o, n = _IN_OFF[name]
        return w[:, o:o + n]
    d = w.shape[0]
    wb = jnp.concatenate([cols(n) for n in _HB_ORDER], axis=1).astype(BF16)
    small = jnp.concatenate([cols("cik"), cols("aa"), cols("ciw"), jnp.zeros((d, 32), w.dtype)], axis=1)
    wf = jnp.concatenate([cols("ak"), cols("ckv"), small, jnp.zeros((d, HF_WIDTH - HF_SMALL - 128), w.dtype)],
                         axis=1).astype(BF16)
    wt = jnp.concatenate([cols("cq"), cols("ciq")], axis=1).astype(BF16).T
    return wb, wf, wt


def _gla_kernel(v_ref, q_ref, g_ref, k_ref, sm_ref, w2_ref, b_ref, ng_ref, o_ref, st_ref, *, nchunk):
    @pl.when(pl.program_id(0) == 0)
    def _():
        st_ref[...] = jnp.zeros_like(st_ref)

    row = lax.broadcasted_iota(I32, (CHUNK, CHUNK), 0)
    col = lax.broadcasted_iota(I32, (CHUNK, CHUNK), 1)
    tri = (col <= row).astype(F32)
    chunks = [slice(c * CHUNK, (c + 1) * CHUNK) for c in range(nchunk)]
    kss = [slice(h * GLA_DK, (h + 1) * GLA_DK) for h in range(GLA_HEADS)]
    vss = [slice(h * GLA_DV, (h + 1) * GLA_DV) for h in range(GLA_HEADS)]

    aa = sm_ref[:, SM_AA:SM_AA + GLA_RANK].astype(BF16)
    z = jnp.dot(aa, w2_ref[...].astype(BF16), preferred_element_type=F32) + b_ref[...]
    log_a = (jnp.minimum(z, 0.0) - jnp.log1p(jnp.exp(-jnp.abs(z)))) * (1.0 / GLA_TAU)
    cums = [jnp.dot(tri, log_a[rs, :], preferred_element_type=F32, precision=lax.Precision.HIGHEST)
            for rs in chunks]
    tots = [cum[CHUNK - 1:CHUNK, :] for cum in cums]
    k_decs = [(k_ref[rs, :] * jnp.exp(tot - cum)).astype(BF16) for rs, cum, tot in zip(chunks, cums, tots)]
    decays = [jnp.exp(tot) for tot in tots]
    upds = [[lax.dot_general(v_ref[rs, vss[h]], k_dec[:, kss[h]], (((0,), (0,)), ((), ())),
                             preferred_element_type=F32) for h in range(GLA_HEADS)]
            for rs, k_dec in zip(chunks, k_decs)]
    states = [[None] * GLA_HEADS for _ in range(nchunk)]
    for h in range(GLA_HEADS):
        st = st_ref[h]
        for c in range(nchunk):
            st = st * decays[c][:, kss[h]] + upds[c][h]
            states[c][h] = st.astype(BF16)
        st_ref[h] = st
    outs = [[lax.dot_general(q_ref[rs, kss[h]], states[c][h], (((1,), (1,)), ((), ())),
                             preferred_element_type=F32) for h in range(GLA_HEADS)]
            for c, rs in enumerate(chunks)]
    for c, rs in enumerate(chunks):
        for h in range(GLA_HEADS):
            o = outs[c][h] * (GLA_DK ** -0.5)
            var = jnp.mean(o * o, axis=-1, keepdims=True)
            y = o * lax.rsqrt(var + RMS_EPS) * ng_ref[...]
            g = g_ref[rs, vss[h]].astype(F32)
            o_ref[rs, vss[h]] = (y * _silu(g)).astype(o_ref.dtype)


def _gla(hb, hf, w2, b, ng, tr):
    s = hb.shape[0]
    nchunk = tr // CHUNK
    return pl.pallas_call(
        functools.partial(_gla_kernel, nchunk=nchunk),
        out_shape=jax.ShapeDtypeStruct((s, GLA_HEADS * GLA_DV), BF16),
        grid=(s // tr,),
        in_specs=[
            pl.BlockSpec((tr, 2048), lambda i: (i, _HB_OFF["av"] // 2048)),
            pl.BlockSpec((tr, 1024), lambda i: (i, _HB_OFF["aq"] // 1024)),
            pl.BlockSpec((tr, 2048), lambda i: (i, _HB_OFF["ag"] // 2048)),
            pl.BlockSpec((tr, 1024), lambda i: (i, HF_AK // 1024)),
            pl.BlockSpec((tr, 128), lambda i: (i, HF_SMALL // 128)),
            pl.BlockSpec((GLA_RANK, 1024), lambda i: (0, 0)),
            pl.BlockSpec((1, 1024), lambda i: (0, 0)),
            pl.BlockSpec((1, GLA_DV), lambda i: (0, 0)),
        ],
        out_specs=pl.BlockSpec((tr, 2048), lambda i: (i, 0)),
        scratch_shapes=[pltpu.VMEM((GLA_HEADS, GLA_DV, GLA_DK), F32)],
        compiler_params=_cparams(("arbitrary",)),
        name="gla",
    )(hb, hb, hb, hf, hf, w2, b, ng)


def _diff_kernel(lam_ref, q_ref, k_ref, v_ref, g_ref, ng_ref, o_ref, acc_ref, sa_ref, sb_ref, *, t, lam_init):
    i = pl.program_id(1)
    lp = lam_ref[...]
    lam = (jnp.exp(jnp.sum(lp[0:1] * lp[1:2], axis=(0, 1), keepdims=True))
           - jnp.exp(jnp.sum(lp[2:3] * lp[3:4], axis=(0, 1), keepdims=True)) + lam_init)
    acc_ref[...] = jnp.zeros_like(acc_ref)
    maps = [slice(mp * DIFF_DQK, (mp + 1) * DIFF_DQK) for mp in range(2)]

    def qk(j, s_ref):
        kt = k_ref[pl.ds(pl.multiple_of(j * t, t), t), :]
        for mp in range(2):
            s_ref[mp] = lax.dot_general(q_ref[:, maps[mp]], kt[:, maps[mp]], (((1,), (1,)), ((), ())),
                                        preferred_element_type=F32)

    def soft_pv(j, s_ref, carry, masked):
        vt = v_ref[pl.ds(pl.multiple_of(j * t, t), t), :]
        out = []
        for mp in range(2):
            m_old, l_old = carry[mp]
            s = s_ref[mp]
            if masked:
                rch = lax.broadcasted_iota(I32, (t, t), 0) // CHUNK
                cch = lax.broadcasted_iota(I32, (t, t), 1) // CHUNK
                s = jnp.where(cch <= rch, s, NEG_BIG)
            m_new = jnp.maximum(m_old, jnp.max(s, axis=-1, keepdims=True))
            p = jnp.exp2(s - m_new)
            alpha = jnp.exp2(m_old - m_new)
            l_new = alpha * l_old + jnp.sum(p, axis=-1, keepdims=True)
            acc_ref[mp] = acc_ref[mp] * alpha + jnp.dot(p.astype(BF16), vt, preferred_element_type=F32)
            out.append((m_new, l_new))
        return tuple(out)

    def pair(jj, carry):
        j0 = 2 * jj
        qk(j0 + 1, sb_ref)
        carry = soft_pv(j0, sa_ref, carry, False)
        qk(j0 + 2, sa_ref)
        return soft_pv(j0 + 1, sb_ref, carry, False)

    def even_tail(carry):
        return soft_pv(i, sa_ref, carry, True)

    def odd_tail(carry):
        qk(i, sb_ref)
        carry = soft_pv(i - 1, sa_ref, carry, False)
        return soft_pv(i, sb_ref, carry, True)

    qk(0, sa_ref)
    init = tuple((jnp.full((t, 1), NEG_BIG, F32), jnp.zeros((t, 1), F32)) for _ in range(2))
    carry = lax.fori_loop(0, i // 2, pair, init)
    (_, l0), (_, l1) = lax.cond(i % 2 == 0, even_tail, odd_tail, carry)
    o = acc_ref[0] / l0 - lam * (acc_ref[1] / l1)
    var = jnp.mean(o * o, axis=-1, keepdims=True)
    y = o * lax.rsqrt(var + RMS_EPS) * ng_ref[...] * (1.0 - lam_init)
    g = g_ref[...].astype(F32)
    o_ref[...] = (y * _silu(g)).astype(o_ref.dtype)


def _diff(hb, lam_p, ng, lam_init, t):
    s = hb.shape[0]
    w = DIFF_DV
    return pl.pallas_call(
        functools.partial(_diff_kernel, t=t, lam_init=lam_init),
        out_shape=jax.ShapeDtypeStruct((s, DIFF_HEADS * DIFF_DV), BF16),
        grid=(DIFF_HEADS, s // t),
        in_specs=[
            pl.BlockSpec((4, DIFF_DQK), lambda h, i: (0, 0)),
            pl.BlockSpec((t, w), lambda h, i: (i, _HB_OFF["bq"] // w + h)),
            pl.BlockSpec((s, w), lambda h, i: (0, _HB_OFF["bk"] // w + h)),
            pl.BlockSpec((s, w), lambda h, i: (0, _HB_OFF["bv"] // w + h)),
            pl.BlockSpec((t, w), lambda h, i: (i, _HB_OFF["bg"] // w + h)),
            pl.BlockSpec((1, DIFF_DV), lambda h, i: (0, 0)),
        ],
        out_specs=pl.BlockSpec((t, w), lambda h, i: (i, h)),
        scratch_shapes=[pltpu.VMEM((2, t, DIFF_DV), F32), pltpu.VMEM((2, t, t), F32), pltpu.VMEM((2, t, t), F32)],
        compiler_params=_cparams(("parallel", "arbitrary")),
        name="diff_attn",
    )(lam_p, hb, hb, hb, hb, ng)


def _spa_norm_kernel(ckv_ref, sm_ref, kvg_ref, lg_ref, lb_ref, cn_ref, cnt_ref, ikn_ref):
    c = ckv_ref[...]
    cn = c * lax.rsqrt(jnp.mean(c * c, axis=-1, keepdims=True) + RMS_EPS) * kvg_ref[...]
    cn_ref[...] = cn.astype(cn_ref.dtype)
    cnt_ref[0:SPA_LAT, :] = cn.T.astype(cnt_ref.dtype)
    pad_rows = cnt_ref.shape[0] - SPA_LAT
    ones_row = lax.broadcasted_iota(I32, (pad_rows, cnt_ref.shape[1]), 0) == 0
    cnt_ref[SPA_LAT:, :] = jnp.where(ones_row, 1.0, 0.0).astype(cnt_ref.dtype)
    ik = sm_ref[:, SM_CIK:SM_CIK + IDX_DIM]
    mu = jnp.mean(ik, axis=-1, keepdims=True)
    d = ik - mu
    var = jnp.mean(d * d, axis=-1, keepdims=True)
    ikn_ref[...] = (d * lax.rsqrt(var + LN_EPS) * lg_ref[...] + lb_ref[...]).astype(ikn_ref.dtype)


def _spa_norm(hf, kvg, lg, lb, tk):
    s = hf.shape[0]
    nt = s // tk
    return pl.pallas_call(
        _spa_norm_kernel,
        out_shape=(jax.ShapeDtypeStruct((nt, tk, SPA_LAT), BF16),
                   jax.ShapeDtypeStruct((nt, SPA_LATP, tk), BF16),
                   jax.ShapeDtypeStruct((nt, tk, IDX_DIM), BF16)),
        grid=(nt,),
        in_specs=[
            pl.BlockSpec((tk, SPA_LAT), lambda i: (i, HF_CKV // SPA_LAT)),
            pl.BlockSpec((tk, 128), lambda i: (i, HF_SMALL // 128)),
            pl.BlockSpec((1, SPA_LAT), lambda i: (0, 0)),
            pl.BlockSpec((1, IDX_DIM), lambda i: (0, 0)),
            pl.BlockSpec((1, IDX_DIM), lambda i: (0, 0)),
        ],
        out_specs=(pl.BlockSpec((None, tk, SPA_LAT), lambda i: (i, 0, 0)),
                   pl.BlockSpec((None, SPA_LATP, tk), lambda i: (i, 0, 0)),
                   pl.BlockSpec((None, tk, IDX_DIM), lambda i: (i, 0, 0))),
        compiler_params=_cparams(("parallel",)),
        name="spa_norm",
    )(hf, hf, kvg, lg, lb)


def _spa_kernel(iq_ref, q8_ref, w_ref, g_ref, ikn_ref, cn_ref, cnt_ref, wuv_ref, o_ref, keys_ref, acc_ref,
                iqt_ref, q8t_ref, sa_ref, sb_ref, *, tq, tk, topk, idx_bits):
    i = pl.program_id(0)
    for h in range(IDX_HEADS):
        iqt_ref[:, h * tq:(h + 1) * tq] = iq_ref[h * IDX_DIM:(h + 1) * IDX_DIM, :]
    for h in range(SPA_HEADS):
        q8t_ref[:, h * tq:(h + 1) * tq] = q8_ref[h * SPA_LAT:(h + 1) * SPA_LAT, :]
    n_t = (i * tq + tq + tk - 1) // tk
    qpos = i * tq + lax.broadcasted_iota(I32, (1, tq), 1)
    qch = qpos // CHUNK
    k_row = jnp.minimum(topk, (qch + 1) * CHUNK)
    hk = tk // 2

    def score_tile(j, masked):
        for half in range(2):
            ks = slice(half * hk, (half + 1) * hk)
            ikt = ikn_ref[j, ks, :]
            sc = jnp.zeros((hk, tq), F32)
            for hp in range(IDX_HEADS // 2):
                r = jnp.dot(ikt, iqt_ref[:, hp * 2 * tq:(hp + 1) * 2 * tq], preferred_element_type=F32)
                for hh in range(2):
                    h = hp * 2 + hh
                    sc = sc + jnp.maximum(r[:, hh * tq:(hh + 1) * tq], 0.0) * w_ref[h:h + 1, :]
            bits = pltpu.bitcast(sc, I32)
            key = bits ^ ((bits >> 31) & 0x7FFFFFFF)
            if masked:
                kch = (j * tk + half * hk + lax.broadcasted_iota(I32, (hk, tq), 0)) // CHUNK
                key = jnp.where(kch <= qch, key, INT_MIN)
            keys_ref[j, ks, :] = key

    def body_a(j, _):
        score_tile(j, False)
        return 0

    lax.fori_loop(0, n_t - 1, body_a, 0)
    score_tile(n_t - 1, True)

    keys_ref[n_t] = jnp.full((tk, tq), INT_MIN, I32)
    n_t2 = (n_t + 1) // 2

    def count_where(pred):
        def body(jj, cnt):
            for j in (2 * jj, 2 * jj + 1):
                ind = jnp.where(pred(keys_ref[j], j), 1, 0).astype(I32)
                cnt = cnt + jnp.sum(ind.reshape(tk // 8, 8, tq), axis=0)
            return cnt
        cnt = lax.fori_loop(0, n_t2, body, jnp.zeros((8, tq), I32))
        return jnp.sum(cnt, axis=0, keepdims=True)

    def bis_cond(st):
        b, _, cnt_a = st
        return jnp.logical_and(b < 32, jnp.max(cnt_a - k_row) > 0)

    def bis_body(st):
        b, a, cnt_a = st
        cand_a = a | lax.shift_left(jnp.int32(1), 31 - b)
        cand = cand_a ^ INT_MIN
        cnt = count_where(lambda kt, j: kt >= cand)
        take = cnt >= k_row
        return b + 1, jnp.where(take, cand_a, a), jnp.where(take, cnt, cnt_a)

    st0 = (jnp.int32(0), jnp.zeros((1, tq), I32), jnp.zeros((1, tq), I32) + n_t * tk)
    st1 = lax.fori_loop(0, BISECT_BLIND_BITS, lambda _, st: bis_body(st), st0)
    _, a_fin, cnt_fin = lax.while_loop(bis_cond, bis_body, st1)
    thr = a_fin ^ INT_MIN

    @pl.when(jnp.max(cnt_fin - k_row) > 0)
    def _():
        need = k_row - count_where(lambda kt, j: kt > thr)

        def idx_of(j):
            return j * tk + lax.broadcasted_iota(I32, (tk, tq), 0)

        def idx_body(b, y):
            cand = y | lax.shift_left(jnp.int32(1), idx_bits - 1 - b)
            cnt = count_where(lambda kt, j: jnp.logical_and(kt == thr, idx_of(j) < cand))
            return jnp.where(cnt < need, cand, y)

        y_keep = lax.fori_loop(0, idx_bits, idx_body, jnp.zeros((1, tq), I32))

        def fix_body(j, _):
            kt = keys_ref[j]
            drop = jnp.logical_and(kt == thr, idx_of(j) > y_keep)
            keys_ref[j] = jnp.where(drop, kt - 1, kt)
            return 0

        lax.fori_loop(0, n_t, fix_body, 0)

    acc_ref[...] = jnp.zeros_like(acc_ref)

    last = keys_ref.shape[0] - 2

    def qk(j, s_ref):
        s_ref[...] = jnp.dot(cn_ref[jnp.minimum(j, last)], q8t_ref[...], preferred_element_type=F32)

    def soft_pv(j, s_ref, m_old):
        madd = jnp.where(keys_ref[j] >= thr, 0.0, NEG_BIG)
        s = s_ref[...] + jnp.concatenate([madd] * SPA_HEADS, axis=1)
        m_new = jnp.maximum(m_old, jnp.max(s, axis=0, keepdims=True))
        p = jnp.exp2(s - m_new)
        alpha = jnp.exp2(m_old - m_new)
        acc_ref[...] = acc_ref[...] * alpha + jnp.dot(cnt_ref[jnp.minimum(j, last)], p.astype(BF16),
                                                      preferred_element_type=F32)
        return m_new

    def body_c(jj, m_run):
        j0 = 2 * jj
        qk(j0 + 1, sb_ref)
        m_run = soft_pv(j0, sa_ref, m_run)
        qk(j0 + 2, sa_ref)
        return soft_pv(j0 + 1, sb_ref, m_run)

    qk(0, sa_ref)
    lax.fori_loop(0, n_t2, body_c, jnp.full((1, SPA_HEADS * tq), NEG_BIG, F32))
    l_fin = acc_ref[SPA_LAT:SPA_LAT + 1, :]

    for h in range(SPA_HEADS):
        cs = slice(h * tq, (h + 1) * tq)
        o_lat = (acc_ref[0:SPA_LAT, cs] / l_fin[:, cs]).T
        oh = jnp.dot(o_lat.astype(BF16), wuv_ref[h], preferred_element_type=F32)
        g = g_ref[:, h * SPA_DV:(h + 1) * SPA_DV].astype(F32)
        o_ref[:, h * SPA_DV:(h + 1) * SPA_DV] = (oh * _silu(g)).astype(o_ref.dtype)


def _spa(hb, ht, wt, ikn, cn, cnt, wuv, tq, tk, topk):
    s = hb.shape[0]
    nq, nt = s // tq, s // tk
    const3 = lambda i: (0, 0, 0)
    return pl.pallas_call(
        functools.partial(_spa_kernel, tq=tq, tk=tk, topk=topk, idx_bits=(s - 1).bit_length()),
        out_shape=jax.ShapeDtypeStruct((s, SPA_HEADS * SPA_DV), BF16),
        grid=(nq,),
        in_specs=[
            pl.BlockSpec((IDX_HEADS * IDX_DIM, tq), lambda i: (HT_CIQ // (IDX_HEADS * IDX_DIM), i)),
            pl.BlockSpec((SPA_HEADS * SPA_LAT, tq), lambda i: (HT_CQ // (SPA_HEADS * SPA_LAT), i)),
            pl.BlockSpec((None, IDX_HEADS, tq), lambda i: (i, 0, 0)),
            pl.BlockSpec((tq, 1024), lambda i: (i, _HB_OFF["cg"] // 1024)),
            pl.BlockSpec((nt, tk, IDX_DIM), const3, pipeline_mode=pl.Buffered(1)),
            pl.BlockSpec((nt, tk, SPA_LAT), const3, pipeline_mode=pl.Buffered(1)),
            pl.BlockSpec((nt, SPA_LATP, tk), const3, pipeline_mode=pl.Buffered(1)),
            pl.BlockSpec((SPA_HEADS, SPA_LAT, SPA_DV), const3, pipeline_mode=pl.Buffered(1)),
        ],
        out_specs=pl.BlockSpec((tq, 1024), lambda i: (i, 0)),
        scratch_shapes=[pltpu.VMEM((nt + 1, tk, tq), I32), pltpu.VMEM((SPA_LATP, SPA_HEADS * tq), F32),
                        pltpu.VMEM((IDX_DIM, IDX_HEADS * tq), BF16), pltpu.VMEM((SPA_LAT, SPA_HEADS * tq), BF16),
                        pltpu.VMEM((tk, SPA_HEADS * tq), F32), pltpu.VMEM((tk, SPA_HEADS * tq), F32)],
        compiler_params=_cparams(("arbitrary",)),
        name="sparse_attn",
    )(ht, ht, wt, hb, ikn, cn, cnt, wuv)


def _out_kernel(oa_ref, ob_ref, oc_ref, x_ref, w_ref, lg_ref, lb_ref, y_ref, yb_ref, *, alpha):
    d = y_ref.shape[1]
    ncol = d // OUT_NC
    na, nb = oa_ref.shape[1], ob_ref.shape[1]
    oa, ob, oc = oa_ref[...], ob_ref[...], oc_ref[...]
    ssum = 0.0
    for c in range(ncol):
        cs = slice(c * OUT_NC, (c + 1) * OUT_NC)
        r = (alpha * x_ref[:, cs]
             + jnp.dot(oa, w_ref[0:na, cs], preferred_element_type=F32)
             + jnp.dot(ob, w_ref[na:na + nb, cs], preferred_element_type=F32)
             + jnp.dot(oc, w_ref[na + nb:, cs], preferred_element_type=F32))
        y_ref[:, cs] = r
        ssum = ssum + jnp.sum(r, axis=-1, keepdims=True)
    mu = ssum / d
    vsum = 0.0
    for c in range(ncol):
        cen = y_ref[:, c * OUT_NC:(c + 1) * OUT_NC] - mu
        vsum = vsum + jnp.sum(cen * cen, axis=-1, keepdims=True)
    rstd = lax.rsqrt(vsum / d + LN_EPS)
    for c in range(ncol):
        cs = slice(c * OUT_NC, (c + 1) * OUT_NC)
        y = (y_ref[:, cs] - mu) * rstd * lg_ref[:, cs] + lb_ref[:, cs]
        y_ref[:, cs] = y
        yb_ref[:, cs] = y.astype(BF16)


def _out_proj(oa, ob, oc, x, w, lg, lb, alpha, tm):
    s, d = x.shape
    row = lambda i: (i, 0)
    const = lambda i: (0, 0)
    return pl.pallas_call(
        functools.partial(_out_kernel, alpha=alpha),
        out_shape=(jax.ShapeDtypeStruct((s, d), F32), jax.ShapeDtypeStruct((s, d), BF16)),
        grid=(s // tm,),
        in_specs=[
            pl.BlockSpec((tm, oa.shape[1]), row),
            pl.BlockSpec((tm, ob.shape[1]), row),
            pl.BlockSpec((tm, oc.shape[1]), row),
            pl.BlockSpec((tm, d), row),
            pl.BlockSpec(w.shape, const, pipeline_mode=pl.Buffered(1)),
            pl.BlockSpec((1, d), const),
            pl.BlockSpec((1, d), const),
        ],
        out_specs=(pl.BlockSpec((tm, d), row), pl.BlockSpec((tm, d), row)),
        compiler_params=_cparams(("parallel",)),
        name="out_proj_ln",
    )(oa, ob, oc, x, w, lg, lb)


def kernel(x, w_in, w_out, gla_w_gate2, gla_b_gate, gla_norm_g, diff_lambda, diff_norm_g,
           spa_kv_norm_g, spa_ik_ln_g, spa_ik_ln_b, spa_w_uv, post_ln_g, post_ln_b):
    bsz, s, d = x.shape
    assert bsz == 1 and d == 4096 and w_in.shape[2] == sum(_IN_WIDTHS)
    depth = w_in.shape[0]
    alpha = (2.0 * depth) ** 0.25
    topk = min(IDX_TOPK_MAX, s // 4)
    tm_in = min(1024, s)
    t_diff = min(512, s)
    tq_s, tk_s = 128, min(512, s)
    nq_s = s // tq_s

    xf = x[0]
    xb = xf.astype(BF16)
    for l in range(depth):
        wb, wf, wqt = _prep_w_in(w_in[l])
        tn = IN_TN
        hb = _matmul(xb, wb, BF16, tm_in, tn,
                     fold=(_HB_OFF["bq"] // tn, (_HB_OFF["bq"] + _IN_OFF["bq"][1]) // tn, _Q_FOLD["bq"]))
        hf = _matmul(xb, wf, F32, tm_in, tn)
        ht = _matmul_t(wqt, xb, BF16, tn, tm_in,
                       fold=(HT_CQ // tn, (HT_CQ + _IN_OFF["cq"][1]) // tn, _Q_FOLD["cq"]))

        o_a = _gla(hb, hf, gla_w_gate2[l], gla_b_gate[l][None], gla_norm_g[l][None], min(256, s))

        lam_init = 0.8 - 0.6 * math.exp(-0.3 * l)
        o_b = _diff(hb, diff_lambda[l], diff_norm_g[l][None], lam_init, t_diff)

        cn, cnt, ikn = _spa_norm(hf, spa_kv_norm_g[l][None], spa_ik_ln_g[l][None], spa_ik_ln_b[l][None], tk_s)
        ciw = lax.slice_in_dim(hf, HF_SMALL + SM_CIW, HF_SMALL + SM_CIW + IDX_HEADS, axis=1)
        wt = (ciw * (IDX_HEADS ** -0.5 * IDX_DIM ** -0.5)).reshape(nq_s, tq_s, IDX_HEADS).transpose(0, 2, 1)
        o_c = _spa(hb, ht, wt, ikn, cn, cnt, spa_w_uv[l].astype(BF16), tq_s, tk_s, topk)

        xf, xb = _out_proj(o_a, o_b, o_c, xf, w_out[l].astype(BF16), post_ln_g[l][None], post_ln_b[l][None],
                           alpha, min(128, s))
    return xf[None]
```

```python
import functools
import math

import jax
import jax.numpy as jnp
from jax import lax
from jax.experimental import pallas as pl
from jax.experimental.pallas import tpu as pltpu

F32 = jnp.float32
BF16 = jnp.bfloat16
I32 = jnp.int32

CHUNK = 64
GLA_HEADS, GLA_DK, GLA_DV, GLA_RANK, GLA_TAU = 4, 256, 512, 16, 16.0
DIFF_HEADS, DIFF_DQK, DIFF_DV = 4, 128, 256
SPA_HEADS, SPA_DV, SPA_LAT = 8, 128, 256
IDX_HEADS, IDX_DIM, IDX_TOPK_MAX = 16, 64, 256
SPA_LATP = SPA_LAT + 16
LN_EPS, RMS_EPS = 1e-5, 1e-6

_IN_WIDTHS = (1024, 1024, 2048, 16, 2048, 1024, 1024, 1024, 1024, 2048, 256, 1024, 64, 16, 1024)
_IN_NAMES = ("aq", "ak", "av", "aa", "ag", "bq", "bk", "bv", "bg", "cq", "ckv", "ciq", "cik", "ciw", "cg")
_IN_OFF = {}
_acc = 0
for _n, _w in zip(_IN_NAMES, _IN_WIDTHS):
    _IN_OFF[_n] = (_acc, _w)
    _acc += _w

_HB_ORDER = ("av", "ag", "aq", "bk", "bg", "cg")
_HB_OFF = {}
_acc = 0
for _n in _HB_ORDER:
    _HB_OFF[_n] = _acc
    _acc += _IN_OFF[_n][1]
HB_WIDTH = _acc
_HT_ORDER = ("cq", "ciq", "bq", "bv")
HT_CQ, HT_CIQ, HT_BQ, HT_BV, HT_WIDTH = 0, 2048, 3072, 4096, 5120
HF_AK, HF_CKV, HF_SMALL, HF_WIDTH = 0, 1024, 1280, 1536
SM_CIK, SM_AA, SM_CIW = 0, 64, 80

_Q_FOLD = {"bq": DIFF_DQK ** -0.5 * math.log2(math.e), "cq": SPA_LAT ** -0.5 * math.log2(math.e)}
IN_TN = 512
BISECT_BLIND_BITS = 24
OUT_NC = 1024
VMEM_LIMIT = 56 * 1024 * 1024
NEG_BIG = -1e30
INT_MIN = -(2 ** 31)


def _cparams(sem):
    return pltpu.CompilerParams(dimension_semantics=sem, vmem_limit_bytes=VMEM_LIMIT)


def _silu(g):
    return g / (1.0 + jnp.exp(-g))


def _tile_scale(folds):
    j = pl.program_id(1)
    scale = jnp.float32(1.0)
    for lo, hi, c in folds:
        scale = jnp.where(jnp.logical_and(j >= lo, j < hi), c, scale)
    return scale


def _matmul_kernel(x_ref, w_ref, o_ref, *, fold):
    acc = jnp.dot(x_ref[...], w_ref[...], preferred_element_type=F32)
    o_ref[...] = (acc * _tile_scale(fold)).astype(o_ref.dtype)


def _matmul(x, w, out_dtype, tm, tn, fold=()):
    m, k = x.shape
    n = w.shape[1]
    return pl.pallas_call(
        functools.partial(_matmul_kernel, fold=fold),
        out_shape=jax.ShapeDtypeStruct((m, n), out_dtype),
        grid=(m // tm, n // tn),
        in_specs=[pl.BlockSpec((tm, k), lambda i, j: (i, 0)),
                  pl.BlockSpec((k, tn), lambda i, j: (0, j))],
        out_specs=pl.BlockSpec((tm, tn), lambda i, j: (i, j)),
        compiler_params=_cparams(("parallel", "parallel")),
        name="in_proj",
    )(x, w)


def _matmul_t_kernel(wt_ref, x_ref, o_ref, *, fold):
    acc = lax.dot_general(wt_ref[...], x_ref[...], (((1,), (1,)), ((), ())), preferred_element_type=F32)
    o_ref[...] = (acc * _tile_scale(fold)).astype(o_ref.dtype)


def _matmul_t(wt, x, out_dtype, tn, tm, fold=()):
    n, k = wt.shape
    m = x.shape[0]
    return pl.pallas_call(
        functools.partial(_matmul_t_kernel, fold=fold),
        out_shape=jax.ShapeDtypeStruct((n, m), out_dtype),
        grid=(m // tm, n // tn),
        in_specs=[pl.BlockSpec((tn, k), lambda i, j: (j, 0)),
                  pl.BlockSpec((tm, k), lambda i, j: (i, 0))],
        out_specs=pl.BlockSpec((tn, tm), lambda i, j: (j, i)),
        compiler_params=_cparams(("parallel", "parallel")),
        name="in_proj_t",
    )(wt, x)


def _prep_w_in(w):
    def cols(name):
        o, n = _IN_OFF[name]
        return w[:, o:o + n]
    d = w.shape[0]
    wb = jnp.concatenate([cols(n) for n in _HB_ORDER], axis=1).astype(BF16)
    small = jnp.concatenate([cols("cik"), cols("aa"), cols("ciw"), jnp.zeros((d, 32), w.dtype)], axis=1)
    wf = jnp.concatenate([cols("ak"), cols("ckv"), small, jnp.zeros((d, HF_WIDTH - HF_SMALL - 128), w.dtype)],
                         axis=1).astype(BF16)
    wt = jnp.concatenate([cols(n) for n in _HT_ORDER], axis=1).astype(BF16).T
    return wb, wf, wt


def _gla_kernel(v_ref, q_ref, g_ref, k_ref, sm_ref, w2_ref, b_ref, ng_ref, o_ref, st_ref, *, nchunk):
    @pl.when(pl.program_id(0) == 0)
    def _():
        st_ref[...] = jnp.zeros_like(st_ref)

    row = lax.broadcasted_iota(I32, (CHUNK, CHUNK), 0)
    col = lax.broadcasted_iota(I32, (CHUNK, CHUNK), 1)
    tri = (col <= row).astype(F32)
    chunks = [slice(c * CHUNK, (c + 1) * CHUNK) for c in range(nchunk)]
    kss = [slice(h * GLA_DK, (h + 1) * GLA_DK) for h in range(GLA_HEADS)]
    vss = [slice(h * GLA_DV, (h + 1) * GLA_DV) for h in range(GLA_HEADS)]

    aa = sm_ref[:, SM_AA:SM_AA + GLA_RANK].astype(BF16)
    z = jnp.dot(aa, w2_ref[...].astype(BF16), preferred_element_type=F32) + b_ref[...]
    log_a = (jnp.minimum(z, 0.0) - jnp.log1p(jnp.exp(-jnp.abs(z)))) * (1.0 / GLA_TAU)
    cums = [jnp.dot(tri, log_a[rs, :], preferred_element_type=F32, precision=lax.Precision.HIGHEST)
            for rs in chunks]
    tots = [cum[CHUNK - 1:CHUNK, :] for cum in cums]
    k_decs = [(k_ref[rs, :] * jnp.exp(tot - cum)).astype(BF16) for rs, cum, tot in zip(chunks, cums, tots)]
    decays = [jnp.exp(tot) for tot in tots]
    upds = [[lax.dot_general(v_ref[rs, vss[h]], k_dec[:, kss[h]], (((0,), (0,)), ((), ())),
                             preferred_element_type=F32) for h in range(GLA_HEADS)]
            for rs, k_dec in zip(chunks, k_decs)]
    states = [[None] * GLA_HEADS for _ in range(nchunk)]
    for h in range(GLA_HEADS):
        st = st_ref[h]
        for c in range(nchunk):
            st = st * decays[c][:, kss[h]] + upds[c][h]
            states[c][h] = st.astype(BF16)
        st_ref[h] = st
    outs = [[lax.dot_general(q_ref[rs, kss[h]], states[c][h], (((1,), (1,)), ((), ())),
                             preferred_element_type=F32) for h in range(GLA_HEADS)]
            for c, rs in enumerate(chunks)]
    for c, rs in enumerate(chunks):
        for h in range(GLA_HEADS):
            o = outs[c][h] * (GLA_DK ** -0.5)
            var = jnp.mean(o * o, axis=-1, keepdims=True)
            y = o * lax.rsqrt(var + RMS_EPS) * ng_ref[...]
            g = g_ref[rs, vss[h]].astype(F32)
            o_ref[rs, vss[h]] = (y * _silu(g)).astype(o_ref.dtype)


def _gla(hb, hf, w2, b, ng, tr):
    s = hb.shape[0]
    nchunk = tr // CHUNK
    return pl.pallas_call(
        functools.partial(_gla_kernel, nchunk=nchunk),
        out_shape=jax.ShapeDtypeStruct((s, GLA_HEADS * GLA_DV), BF16),
        grid=(s // tr,),
        in_specs=[
            pl.BlockSpec((tr, 2048), lambda i: (i, _HB_OFF["av"] // 2048)),
            pl.BlockSpec((tr, 1024), lambda i: (i, _HB_OFF["aq"] // 1024)),
            pl.BlockSpec((tr, 2048), lambda i: (i, _HB_OFF["ag"] // 2048)),
            pl.BlockSpec((tr, 1024), lambda i: (i, HF_AK // 1024)),
            pl.BlockSpec((tr, 128), lambda i: (i, HF_SMALL // 128)),
            pl.BlockSpec((GLA_RANK, 1024), lambda i: (0, 0)),
            pl.BlockSpec((1, 1024), lambda i: (0, 0)),
            pl.BlockSpec((1, GLA_DV), lambda i: (0, 0)),
        ],
        out_specs=pl.BlockSpec((tr, 2048), lambda i: (i, 0)),
        scratch_shapes=[pltpu.VMEM((GLA_HEADS, GLA_DV, GLA_DK), F32)],
        compiler_params=_cparams(("arbitrary",)),
        name="gla",
    )(hb, hb, hb, hf, hf, w2, b, ng)


def _diff_kernel(lam_ref, q_ref, k_ref, v_ref, g_ref, ng_ref, o_ref, acc_ref, sa_ref, sb_ref, *, t, lam_init):
    i = pl.program_id(1)
    lp = lam_ref[...]
    lam = (jnp.exp(jnp.sum(lp[0:1] * lp[1:2], axis=(0, 1), keepdims=True))
           - jnp.exp(jnp.sum(lp[2:3] * lp[3:4], axis=(0, 1), keepdims=True)) + lam_init)
    acc_ref[...] = jnp.zeros_like(acc_ref)
    maps = [slice(mp * DIFF_DQK, (mp + 1) * DIFF_DQK) for mp in range(2)]

    def qk(j, s_ref):
        kt = k_ref[pl.ds(pl.multiple_of(j * t, t), t), :]
        for mp in range(2):
            s_ref[mp] = jnp.dot(kt[:, maps[mp]], q_ref[maps[mp], :], preferred_element_type=F32)

    def soft_pv(j, s_ref, carry, masked):
        vt = v_ref[:, pl.ds(pl.multiple_of(j * t, t), t)]
        out = []
        for mp in range(2):
            m_old, l_old = carry[mp]
            s = s_ref[mp]
            if masked:
                kch = lax.broadcasted_iota(I32, (t, t), 0) // CHUNK
                qch = lax.broadcasted_iota(I32, (t, t), 1) // CHUNK
                s = jnp.where(kch <= qch, s, NEG_BIG)
            m_new = jnp.maximum(m_old, jnp.max(s, axis=0, keepdims=True))
            p = jnp.exp2(s - m_new)
            alpha = jnp.exp2(m_old - m_new)
            l_new = alpha * l_old + jnp.sum(p, axis=0, keepdims=True)
            acc_ref[mp] = acc_ref[mp] * alpha + jnp.dot(vt, p.astype(BF16), preferred_element_type=F32)
            out.append((m_new, l_new))
        return tuple(out)

    def pair(jj, carry):
        j0 = 2 * jj
        qk(j0 + 1, sb_ref)
        carry = soft_pv(j0, sa_ref, carry, False)
        qk(j0 + 2, sa_ref)
        return soft_pv(j0 + 1, sb_ref, carry, False)

    def even_tail(carry):
        return soft_pv(i, sa_ref, carry, True)

    def odd_tail(carry):
        qk(i, sb_ref)
        carry = soft_pv(i - 1, sa_ref, carry, False)
        return soft_pv(i, sb_ref, carry, True)

    qk(0, sa_ref)
    init = tuple((jnp.full((1, t), NEG_BIG, F32), jnp.zeros((1, t), F32)) for _ in range(2))
    (_, l0), (_, l1) = lax.cond(i % 2 == 0, even_tail, odd_tail, lax.fori_loop(0, i // 2, pair, init))
    o = (acc_ref[0] / l0 - lam * (acc_ref[1] / l1)).T
    var = jnp.mean(o * o, axis=-1, keepdims=True)
    y = o * lax.rsqrt(var + RMS_EPS) * ng_ref[...] * (1.0 - lam_init)
    g = g_ref[...].astype(F32)
    o_ref[...] = (y * _silu(g)).astype(o_ref.dtype)


def _diff(hb, ht, lam_p, ng, lam_init, t):
    s = hb.shape[0]
    w = DIFF_DV
    return pl.pallas_call(
        functools.partial(_diff_kernel, t=t, lam_init=lam_init),
        out_shape=jax.ShapeDtypeStruct((s, DIFF_HEADS * DIFF_DV), BF16),
        grid=(DIFF_HEADS, s // t),
        in_specs=[
            pl.BlockSpec((4, DIFF_DQK), lambda h, i: (0, 0)),
            pl.BlockSpec((w, t), lambda h, i: (HT_BQ // w + h, i)),
            pl.BlockSpec((s, w), lambda h, i: (0, _HB_OFF["bk"] // w + h)),
            pl.BlockSpec((w, s), lambda h, i: (HT_BV // w + h, 0)),
            pl.BlockSpec((t, w), lambda h, i: (i, _HB_OFF["bg"] // w + h)),
            pl.BlockSpec((1, DIFF_DV), lambda h, i: (0, 0)),
        ],
        out_specs=pl.BlockSpec((t, w), lambda h, i: (i, h)),
        scratch_shapes=[pltpu.VMEM((2, DIFF_DV, t), F32), pltpu.VMEM((2, t, t), F32), pltpu.VMEM((2, t, t), F32)],
        compiler_params=_cparams(("parallel", "arbitrary")),
        name="diff_attn",
    )(lam_p, ht, hb, ht, hb, ng)


def _spa_norm_kernel(ckv_ref, sm_ref, kvg_ref, lg_ref, lb_ref, cn_ref, cnt_ref, ikn_ref):
    c = ckv_ref[...]
    cn = c * lax.rsqrt(jnp.mean(c * c, axis=-1, keepdims=True) + RMS_EPS) * kvg_ref[...]
    cn_ref[...] = cn.astype(cn_ref.dtype)
    cnt_ref[0:SPA_LAT, :] = cn.T.astype(cnt_ref.dtype)
    pad_rows = cnt_ref.shape[0] - SPA_LAT
    ones_row = lax.broadcasted_iota(I32, (pad_rows, cnt_ref.shape[1]), 0) == 0
    cnt_ref[SPA_LAT:, :] = jnp.where(ones_row, 1.0, 0.0).astype(cnt_ref.dtype)
    ik = sm_ref[:, SM_CIK:SM_CIK + IDX_DIM]
    mu = jnp.mean(ik, axis=-1, keepdims=True)
    d = ik - mu
    var = jnp.mean(d * d, axis=-1, keepdims=True)
    ikn_ref[...] = (d * lax.rsqrt(var + LN_EPS) * lg_ref[...] + lb_ref[...]).astype(ikn_ref.dtype)


def _spa_norm(hf, kvg, lg, lb, tk):
    s = hf.shape[0]
    nt = s // tk
    return pl.pallas_call(
        _spa_norm_kernel,
        out_shape=(jax.ShapeDtypeStruct((nt, tk, SPA_LAT), BF16),
                   jax.ShapeDtypeStruct((nt, SPA_LATP, tk), BF16),
                   jax.ShapeDtypeStruct((nt, tk, IDX_DIM), BF16)),
        grid=(nt,),
        in_specs=[
            pl.BlockSpec((tk, SPA_LAT), lambda i: (i, HF_CKV // SPA_LAT)),
            pl.BlockSpec((tk, 128), lambda i: (i, HF_SMALL // 128)),
            pl.BlockSpec((1, SPA_LAT), lambda i: (0, 0)),
            pl.BlockSpec((1, IDX_DIM), lambda i: (0, 0)),
            pl.BlockSpec((1, IDX_DIM), lambda i: (0, 0)),
        ],
        out_specs=(pl.BlockSpec((None, tk, SPA_LAT), lambda i: (i, 0, 0)),
                   pl.BlockSpec((None, SPA_LATP, tk), lambda i: (i, 0, 0)),
                   pl.BlockSpec((None, tk, IDX_DIM), lambda i: (i, 0, 0))),
        compiler_params=_cparams(("parallel",)),
        name="spa_norm",
    )(hf, hf, kvg, lg, lb)


def _spa_kernel(iq_ref, q8_ref, w_ref, g_ref, ikn_ref, cn_ref, cnt_ref, wuv_ref, o_ref, keys_ref, acc_ref,
                iqt_ref, q8t_ref, sa_ref, sb_ref, *, tq, tk, topk, idx_bits):
    i = pl.program_id(0)
    for h in range(IDX_HEADS):
        iqt_ref[:, h * tq:(h + 1) * tq] = iq_ref[h * IDX_DIM:(h + 1) * IDX_DIM, :]
    for h in range(SPA_HEADS):
        q8t_ref[:, h * tq:(h + 1) * tq] = q8_ref[h * SPA_LAT:(h + 1) * SPA_LAT, :]
    n_t = (i * tq + tq + tk - 1) // tk
    qpos = i * tq + lax.broadcasted_iota(I32, (1, tq), 1)
    qch = qpos // CHUNK
    k_row = jnp.minimum(topk, (qch + 1) * CHUNK)
    hk = tk // 2

    def score_tile(j, masked):
        for half in range(2):
            ks = slice(half * hk, (half + 1) * hk)
            ikt = ikn_ref[j, ks, :]
            sc = jnp.zeros((hk, tq), F32)
            for hp in range(IDX_HEADS // 2):
                r = jnp.dot(ikt, iqt_ref[:, hp * 2 * tq:(hp + 1) * 2 * tq], preferred_element_type=F32)
                for hh in range(2):
                    h = hp * 2 + hh
                    sc = sc + jnp.maximum(r[:, hh * tq:(hh + 1) * tq], 0.0) * w_ref[h:h + 1, :]
            bits = pltpu.bitcast(sc, I32)
            key = bits ^ ((bits >> 31) & 0x7FFFFFFF)
            if masked:
                kch = (j * tk + half * hk + lax.broadcasted_iota(I32, (hk, tq), 0)) // CHUNK
                key = jnp.where(kch <= qch, key, INT_MIN)
            keys_ref[j, ks, :] = key

    def body_a(j, _):
        score_tile(j, False)
        return 0

    lax.fori_loop(0, n_t - 1, body_a, 0)
    score_tile(n_t - 1, True)

    keys_ref[n_t] = jnp.full((tk, tq), INT_MIN, I32)
    n_t2 = (n_t + 1) // 2

    def count_where(pred):
        def body(jj, cnt):
            for j in (2 * jj, 2 * jj + 1):
                ind = jnp.where(pred(keys_ref[j], j), 1, 0).astype(I32)
                cnt = cnt + jnp.sum(ind.reshape(tk // 8, 8, tq), axis=0)
            return cnt
        cnt = lax.fori_loop(0, n_t2, body, jnp.zeros((8, tq), I32))
        return jnp.sum(cnt, axis=0, keepdims=True)

    def bis_cond(st):
        b, _, cnt_a = st
        return jnp.logical_and(b < 32, jnp.max(cnt_a - k_row) > 0)

    def bis_body(st):
        b, a, cnt_a = st
        cand_a = a | lax.shift_left(jnp.int32(1), 31 - b)
        cand = cand_a ^ INT_MIN
        cnt = count_where(lambda kt, j: kt >= cand)
        take = cnt >= k_row
        return b + 1, jnp.where(take, cand_a, a), jnp.where(take, cnt, cnt_a)

    st0 = (jnp.int32(0), jnp.zeros((1, tq), I32), jnp.zeros((1, tq), I32) + n_t * tk)
    st1 = lax.fori_loop(0, BISECT_BLIND_BITS, lambda _, st: bis_body(st), st0)
    _, a_fin, cnt_fin = lax.while_loop(bis_cond, bis_body, st1)
    thr = a_fin ^ INT_MIN

    @pl.when(jnp.max(cnt_fin - k_row) > 0)
    def _():
        need = k_row - count_where(lambda kt, j: kt > thr)

        def idx_of(j):
            return j * tk + lax.broadcasted_iota(I32, (tk, tq), 0)

        def idx_body(b, y):
            cand = y | lax.shift_left(jnp.int32(1), idx_bits - 1 - b)
            cnt = count_where(lambda kt, j: jnp.logical_and(kt == thr, idx_of(j) < cand))
            return jnp.where(cnt < need, cand, y)

        y_keep = lax.fori_loop(0, idx_bits, idx_body, jnp.zeros((1, tq), I32))

        def fix_body(j, _):
            kt = keys_ref[j]
            drop = jnp.logical_and(kt == thr, idx_of(j) > y_keep)
            keys_ref[j] = jnp.where(drop, kt - 1, kt)
            return 0

        lax.fori_loop(0, n_t, fix_body, 0)

    acc_ref[...] = jnp.zeros_like(acc_ref)

    last = keys_ref.shape[0] - 2

    def qk(j, s_ref):
        s_ref[...] = jnp.dot(cn_ref[jnp.minimum(j, last)], q8t_ref[...], preferred_element_type=F32)

    def soft_pv(j, s_ref, m_old):
        madd = jnp.where(keys_ref[j] >= thr, 0.0, NEG_BIG)
        s = s_ref[...] + jnp.concatenate([madd] * SPA_HEADS, axis=1)
        m_new = jnp.maximum(m_old, jnp.max(s, axis=0, keepdims=True))
        p = jnp.exp2(s - m_new)
        alpha = jnp.exp2(m_old - m_new)
        acc_ref[...] = acc_ref[...] * alpha + jnp.dot(cnt_ref[jnp.minimum(j, last)], p.astype(BF16),
                                                      preferred_element_type=F32)
        return m_new

    def body_c(jj, m_run):
        j0 = 2 * jj
        qk(j0 + 1, sb_ref)
        m_run = soft_pv(j0, sa_ref, m_run)
        qk(j0 + 2, sa_ref)
        return soft_pv(j0 + 1, sb_ref, m_run)

    qk(0, sa_ref)
    lax.fori_loop(0, n_t2, body_c, jnp.full((1, SPA_HEADS * tq), NEG_BIG, F32))
    l_fin = acc_ref[SPA_LAT:SPA_LAT + 1, :]

    for h in range(SPA_HEADS):
        cs = slice(h * tq, (h + 1) * tq)
        o_lat = (acc_ref[0:SPA_LAT, cs] / l_fin[:, cs]).T
        oh = jnp.dot(o_lat.astype(BF16), wuv_ref[h], preferred_element_type=F32)
        g = g_ref[:, h * SPA_DV:(h + 1) * SPA_DV].astype(F32)
        o_ref[:, h * SPA_DV:(h + 1) * SPA_DV] = (oh * _silu(g)).astype(o_ref.dtype)


def _spa(hb, ht, wt, ikn, cn, cnt, wuv, tq, tk, topk):
    s = hb.shape[0]
    nq, nt = s // tq, s // tk
    const3 = lambda i: (0, 0, 0)
    return pl.pallas_call(
        functools.partial(_spa_kernel, tq=tq, tk=tk, topk=topk, idx_bits=(s - 1).bit_length()),
        out_shape=jax.ShapeDtypeStruct((s, SPA_HEADS * SPA_DV), BF16),
        grid=(nq,),
        in_specs=[
            pl.BlockSpec((IDX_HEADS * IDX_DIM, tq), lambda i: (HT_CIQ // (IDX_HEADS * IDX_DIM), i)),
            pl.BlockSpec((SPA_HEADS * SPA_LAT, tq), lambda i: (HT_CQ // (SPA_HEADS * SPA_LAT), i)),
            pl.BlockSpec((None, IDX_HEADS, tq), lambda i: (i, 0, 0)),
            pl.BlockSpec((tq, 1024), lambda i: (i, _HB_OFF["cg"] // 1024)),
            pl.BlockSpec((nt, tk, IDX_DIM), const3, pipeline_mode=pl.Buffered(1)),
            pl.BlockSpec((nt, tk, SPA_LAT), const3, pipeline_mode=pl.Buffered(1)),
            pl.BlockSpec((nt, SPA_LATP, tk), const3, pipeline_mode=pl.Buffered(1)),
            pl.BlockSpec((SPA_HEADS, SPA_LAT, SPA_DV), const3, pipeline_mode=pl.Buffered(1)),
        ],
        out_specs=pl.BlockSpec((tq, 1024), lambda i: (i, 0)),
        scratch_shapes=[pltpu.VMEM((nt + 1, tk, tq), I32), pltpu.VMEM((SPA_LATP, SPA_HEADS * tq), F32),
                        pltpu.VMEM((IDX_DIM, IDX_HEADS * tq), BF16), pltpu.VMEM((SPA_LAT, SPA_HEADS * tq), BF16),
                        pltpu.VMEM((tk, SPA_HEADS * tq), F32), pltpu.VMEM((tk, SPA_HEADS * tq), F32)],
        compiler_params=_cparams(("arbitrary",)),
        name="sparse_attn",
    )(ht, ht, wt, hb, ikn, cn, cnt, wuv)


def _out_kernel(oa_ref, ob_ref, oc_ref, x_ref, w_ref, lg_ref, lb_ref, y_ref, yb_ref, *, alpha):
    d = y_ref.shape[1]
    ncol = d // OUT_NC
    na, nb = oa_ref.shape[1], ob_ref.shape[1]
    oa, ob, oc = oa_ref[...], ob_ref[...], oc_ref[...]
    ssum = 0.0
    for c in range(ncol):
        cs = slice(c * OUT_NC, (c + 1) * OUT_NC)
        r = (alpha * x_ref[:, cs]
             + jnp.dot(oa, w_ref[0:na, cs], preferred_element_type=F32)
             + jnp.dot(ob, w_ref[na:na + nb, cs], preferred_element_type=F32)
             + jnp.dot(oc, w_ref[na + nb:, cs], preferred_element_type=F32))
        y_ref[:, cs] = r
        ssum = ssum + jnp.sum(r, axis=-1, keepdims=True)
    mu = ssum / d
    vsum = 0.0
    for c in range(ncol):
        cen = y_ref[:, c * OUT_NC:(c + 1) * OUT_NC] - mu
        vsum = vsum + jnp.sum(cen * cen, axis=-1, keepdims=True)
    rstd = lax.rsqrt(vsum / d + LN_EPS)
    for c in range(ncol):
        cs = slice(c * OUT_NC, (c + 1) * OUT_NC)
        y = (y_ref[:, cs] - mu) * rstd * lg_ref[:, cs] + lb_ref[:, cs]
        y_ref[:, cs] = y
        yb_ref[:, cs] = y.astype(BF16)


def _out_proj(oa, ob, oc, x, w, lg, lb, alpha, tm):
    s, d = x.shape
    row = lambda i: (i, 0)
    const = lambda i: (0, 0)
    return pl.pallas_call(
        functools.partial(_out_kernel, alpha=alpha),
        out_shape=(jax.ShapeDtypeStruct((s, d), F32), jax.ShapeDtypeStruct((s, d), BF16)),
        grid=(s // tm,),
        in_specs=[
            pl.BlockSpec((tm, oa.shape[1]), row),
            pl.BlockSpec((tm, ob.shape[1]), row),
            pl.BlockSpec((tm, oc.shape[1]), row),
            pl.BlockSpec((tm, d), row),
            pl.BlockSpec(w.shape, const, pipeline_mode=pl.Buffered(1)),
            pl.BlockSpec((1, d), const),
            pl.BlockSpec((1, d), const),
        ],
        out_specs=(pl.BlockSpec((tm, d), row), pl.BlockSpec((tm, d), row)),
        compiler_params=_cparams(("parallel",)),
        name="out_proj_ln",
    )(oa, ob, oc, x, w, lg, lb)


def kernel(x, w_in, w_out, gla_w_gate2, gla_b_gate, gla_norm_g, diff_lambda, diff_norm_g,
           spa_kv_norm_g, spa_ik_ln_g, spa_ik_ln_b, spa_w_uv, post_ln_g, post_ln_b):
    bsz, s, d = x.shape
    assert bsz == 1 and d == 4096 and w_in.shape[2] == sum(_IN_WIDTHS)
    depth = w_in.shape[0]
    alpha = (2.0 * depth) ** 0.25
    topk = min(IDX_TOPK_MAX, s // 4)
    tm_in = min(1024, s)
    t_diff = min(512, s)
    tq_s, tk_s = 128, min(512, s)
    nq_s = s // tq_s

    xf = x[0]
    xb = xf.astype(BF16)
    for l in range(depth):
        wb, wf, wqt = _prep_w_in(w_in[l])
        tn = IN_TN
        hb = _matmul(xb, wb, BF16, tm_in, tn)
        hf = _matmul(xb, wf, F32, tm_in, tn)
        ht = _matmul_t(wqt, xb, BF16, tn, tm_in,
                       fold=((HT_CQ // tn, (HT_CQ + _IN_OFF["cq"][1]) // tn, _Q_FOLD["cq"]),
                             (HT_BQ // tn, (HT_BQ + _IN_OFF["bq"][1]) // tn, _Q_FOLD["bq"])))

        o_a = _gla(hb, hf, gla_w_gate2[l], gla_b_gate[l][None], gla_norm_g[l][None], min(256, s))

        lam_init = 0.8 - 0.6 * math.exp(-0.3 * l)
        o_b = _diff(hb, ht, diff_lambda[l], diff_norm_g[l][None], lam_init, t_diff)

        cn, cnt, ikn = _spa_norm(hf, spa_kv_norm_g[l][None], spa_ik_ln_g[l][None], spa_ik_ln_b[l][None], tk_s)
        ciw = lax.slice_in_dim(hf, HF_SMALL + SM_CIW, HF_SMALL + SM_CIW + IDX_HEADS, axis=1)
        wt = (ciw * (IDX_HEADS ** -0.5 * IDX_DIM ** -0.5)).reshape(nq_s, tq_s, IDX_HEADS).transpose(0, 2, 1)
        o_c = _spa(hb, ht, wt, ikn, cn, cnt, spa_w_uv[l].astype(BF16), tq_s, tk_s, topk)

        xf, xb = _out_proj(o_a, o_b, o_c, xf, w_out[l].astype(BF16), post_ln_g[l][None], post_ln_b[l][None],
                           alpha, min(128, s))
    return xf[None]
```

```python
import functools
import math

import jax
import jax.numpy as jnp
from jax import lax
from jax.experimental import pallas as pl
from jax.experimental.pallas import tpu as pltpu

F32 = jnp.float32
BF16 = jnp.bfloat16
I32 = jnp.int32

CHUNK = 64
GLA_HEADS, GLA_DK, GLA_DV, GLA_RANK, GLA_TAU = 4, 256, 512, 16, 16.0
DIFF_HEADS, DIFF_DQK, DIFF_DV = 4, 128, 256
SPA_HEADS, SPA_DV, SPA_LAT = 8, 128, 256
IDX_HEADS, IDX_DIM, IDX_TOPK_MAX = 16, 64, 256
SPA_LATP = SPA_LAT + 16
LN_EPS, RMS_EPS = 1e-5, 1e-6

_IN_WIDTHS = (1024, 1024, 2048, 16, 2048, 1024, 1024, 1024, 1024, 2048, 256, 1024, 64, 16, 1024)
_IN_NAMES = ("aq", "ak", "av", "aa", "ag", "bq", "bk", "bv", "bg", "cq", "ckv", "ciq", "cik", "ciw", "cg")
_IN_OFF = {}
_acc = 0
for _n, _w in zip(_IN_NAMES, _IN_WIDTHS):
    _IN_OFF[_n] = (_acc, _w)
    _acc += _w

_HB_ORDER = ("av", "ag", "aq", "bk", "bg", "cg")
_HB_OFF = {}
_acc = 0
for _n in _HB_ORDER:
    _HB_OFF[_n] = _acc
    _acc += _IN_OFF[_n][1]
HB_WIDTH = _acc
_HT_ORDER = ("cq", "ciq", "bq", "bv")
HT_CQ, HT_CIQ, HT_BQ, HT_BV, HT_WIDTH = 0, 2048, 3072, 4096, 5120
HF_AK, HF_CKV, HF_SMALL, HF_WIDTH = 0, 1024, 1280, 1536
SM_CIK, SM_AA, SM_CIW = 0, 64, 80

_Q_FOLD = {"bq": DIFF_DQK ** -0.5 * math.log2(math.e), "cq": SPA_LAT ** -0.5 * math.log2(math.e)}
IN_TN = 512
BISECT_BLIND_BITS = 24
OUT_NC = 1024
VMEM_LIMIT = 56 * 1024 * 1024
NEG_BIG = -1e30
INT_MIN = -(2 ** 31)
TOP16_MASK = -(2 ** 16)


def _cparams(sem):
    return pltpu.CompilerParams(dimension_semantics=sem, vmem_limit_bytes=VMEM_LIMIT)


def _silu(g):
    return g / (1.0 + jnp.exp(-g))


def _tile_scale(folds):
    j = pl.program_id(1)
    scale = jnp.float32(1.0)
    for lo, hi, c in folds:
        scale = jnp.where(jnp.logical_and(j >= lo, j < hi), c, scale)
    return scale


def _matmul_kernel(x_ref, w_ref, o_ref, *, fold):
    acc = jnp.dot(x_ref[...], w_ref[...], preferred_element_type=F32)
    o_ref[...] = (acc * _tile_scale(fold)).astype(o_ref.dtype)


def _matmul(x, w, out_dtype, tm, tn, fold=()):
    m, k = x.shape
    n = w.shape[1]
    return pl.pallas_call(
        functools.partial(_matmul_kernel, fold=fold),
        out_shape=jax.ShapeDtypeStruct((m, n), out_dtype),
        grid=(m // tm, n // tn),
        in_specs=[pl.BlockSpec((tm, k), lambda i, j: (i, 0)),
                  pl.BlockSpec((k, tn), lambda i, j: (0, j))],
        out_specs=pl.BlockSpec((tm, tn), lambda i, j: (i, j)),
        compiler_params=_cparams(("parallel", "parallel")),
        name="in_proj",
    )(x, w)


def _matmul_t_kernel(wt_ref, x_ref, o_ref, *, fold):
    acc = lax.dot_general(wt_ref[...], x_ref[...], (((1,), (1,)), ((), ())), preferred_element_type=F32)
    o_ref[...] = (acc * _tile_scale(fold)).astype(o_ref.dtype)


def _matmul_t(wt, x, out_dtype, tn, tm, fold=()):
    n, k = wt.shape
    m = x.shape[0]
    return pl.pallas_call(
        functools.partial(_matmul_t_kernel, fold=fold),
        out_shape=jax.ShapeDtypeStruct((n, m), out_dtype),
        grid=(m // tm, n // tn),
        in_specs=[pl.BlockSpec((tn, k), lambda i, j: (j, 0)),
                  pl.BlockSpec((tm, k), lambda i, j: (i, 0))],
        out_specs=pl.BlockSpec((tn, tm), lambda i, j: (j, i)),
        compiler_params=_cparams(("parallel", "parallel")),
        name="in_proj_t",
    )(wt, x)


def _prep_w_in(w):
    def cols(name):
        o, n = _IN_OFF[name]
        return w[:, o:o + n]
    d = w.shape[0]
    wb = jnp.concatenate([cols(n) for n in _HB_ORDER], axis=1).astype(BF16)
    small = jnp.concatenate([cols("cik"), cols("aa"), cols("ciw"), jnp.zeros((d, 32), w.dtype)], axis=1)
    wf = jnp.concatenate([cols("ak"), cols("ckv"), small, jnp.zeros((d, HF_WIDTH - HF_SMALL - 128), w.dtype)],
                         axis=1).astype(BF16)
    wt = jnp.concatenate([cols(n) for n in _HT_ORDER], axis=1).astype(BF16).T
    return wb, wf, wt


def _gla_kernel(v_ref, q_ref, g_ref, k_ref, sm_ref, w2_ref, b_ref, ng_ref, o_ref, st_ref, *, nchunk):
    @pl.when(pl.program_id(0) == 0)
    def _():
        st_ref[...] = jnp.zeros_like(st_ref)

    row = lax.broadcasted_iota(I32, (CHUNK, CHUNK), 0)
    col = lax.broadcasted_iota(I32, (CHUNK, CHUNK), 1)
    tri = (col <= row).astype(F32)
    chunks = [slice(c * CHUNK, (c + 1) * CHUNK) for c in range(nchunk)]
    kss = [slice(h * GLA_DK, (h + 1) * GLA_DK) for h in range(GLA_HEADS)]
    vss = [slice(h * GLA_DV, (h + 1) * GLA_DV) for h in range(GLA_HEADS)]

    aa = sm_ref[:, SM_AA:SM_AA + GLA_RANK].astype(BF16)
    z = jnp.dot(aa, w2_ref[...].astype(BF16), preferred_element_type=F32) + b_ref[...]
    log_a = (jnp.minimum(z, 0.0) - jnp.log1p(jnp.exp(-jnp.abs(z)))) * (1.0 / GLA_TAU)
    cums = [jnp.dot(tri, log_a[rs, :], preferred_element_type=F32, precision=lax.Precision.HIGHEST)
            for rs in chunks]
    tots = [cum[CHUNK - 1:CHUNK, :] for cum in cums]
    k_decs = [(k_ref[rs, :] * jnp.exp(tot - cum)).astype(BF16) for rs, cum, tot in zip(chunks, cums, tots)]
    decays = [jnp.exp(tot) for tot in tots]
    upds = [[lax.dot_general(v_ref[rs, vss[h]], k_dec[:, kss[h]], (((0,), (0,)), ((), ())),
                             preferred_element_type=F32) for h in range(GLA_HEADS)]
            for rs, k_dec in zip(chunks, k_decs)]
    states = [[None] * GLA_HEADS for _ in range(nchunk)]
    for h in range(GLA_HEADS):
        st = st_ref[h]
        for c in range(nchunk):
            st = st * decays[c][:, kss[h]] + upds[c][h]
            states[c][h] = st.astype(BF16)
        st_ref[h] = st
    outs = [[lax.dot_general(q_ref[rs, kss[h]], states[c][h], (((1,), (1,)), ((), ())),
                             preferred_element_type=F32) for h in range(GLA_HEADS)]
            for c, rs in enumerate(chunks)]
    for c, rs in enumerate(chunks):
        for h in range(GLA_HEADS):
            o = outs[c][h] * (GLA_DK ** -0.5)
            var = jnp.mean(o * o, axis=-1, keepdims=True)
            y = o * lax.rsqrt(var + RMS_EPS) * ng_ref[...]
            g = g_ref[rs, vss[h]].astype(F32)
            o_ref[rs, vss[h]] = (y * _silu(g)).astype(o_ref.dtype)


def _gla(hb, hf, w2, b, ng, tr):
    s = hb.shape[0]
    nchunk = tr // CHUNK
    return pl.pallas_call(
        functools.partial(_gla_kernel, nchunk=nchunk),
        out_shape=jax.ShapeDtypeStruct((s, GLA_HEADS * GLA_DV), BF16),
        grid=(s // tr,),
        in_specs=[
            pl.BlockSpec((tr, 2048), lambda i: (i, _HB_OFF["av"] // 2048)),
            pl.BlockSpec((tr, 1024), lambda i: (i, _HB_OFF["aq"] // 1024)),
            pl.BlockSpec((tr, 2048), lambda i: (i, _HB_OFF["ag"] // 2048)),
            pl.BlockSpec((tr, 1024), lambda i: (i, HF_AK // 1024)),
            pl.BlockSpec((tr, 128), lambda i: (i, HF_SMALL // 128)),
            pl.BlockSpec((GLA_RANK, 1024), lambda i: (0, 0)),
            pl.BlockSpec((1, 1024), lambda i: (0, 0)),
            pl.BlockSpec((1, GLA_DV), lambda i: (0, 0)),
        ],
        out_specs=pl.BlockSpec((tr, 2048), lambda i: (i, 0)),
        scratch_shapes=[pltpu.VMEM((GLA_HEADS, GLA_DV, GLA_DK), F32)],
        compiler_params=_cparams(("arbitrary",)),
        name="gla",
    )(hb, hb, hb, hf, hf, w2, b, ng)


def _diff_kernel(lam_ref, q_ref, k_ref, v_ref, g_ref, ng_ref, o_ref, acc_ref, sa_ref, sb_ref, *, t, lam_init):
    i = pl.program_id(1)
    lp = lam_ref[...]
    lam = (jnp.exp(jnp.sum(lp[0:1] * lp[1:2], axis=(0, 1), keepdims=True))
           - jnp.exp(jnp.sum(lp[2:3] * lp[3:4], axis=(0, 1), keepdims=True)) + lam_init)
    acc_ref[...] = jnp.zeros_like(acc_ref)
    maps = [slice(mp * DIFF_DQK, (mp + 1) * DIFF_DQK) for mp in range(2)]

    def qk(j, s_ref):
        kt = k_ref[pl.ds(pl.multiple_of(j * t, t), t), :]
        for mp in range(2):
            s_ref[mp] = jnp.dot(kt[:, maps[mp]], q_ref[maps[mp], :], preferred_element_type=F32)

    def soft_pv(j, s_ref, carry, masked):
        vt = v_ref[:, pl.ds(pl.multiple_of(j * t, t), t)]
        out = []
        for mp in range(2):
            m_old, l_old = carry[mp]
            s = s_ref[mp]
            if masked:
                kch = lax.broadcasted_iota(I32, (t, t), 0) // CHUNK
                qch = lax.broadcasted_iota(I32, (t, t), 1) // CHUNK
                s = jnp.where(kch <= qch, s, NEG_BIG)
            m_new = jnp.maximum(m_old, jnp.max(s, axis=0, keepdims=True))
            p = jnp.exp2(s - m_new)
            alpha = jnp.exp2(m_old - m_new)
            l_new = alpha * l_old + jnp.sum(p, axis=0, keepdims=True)
            acc_ref[mp] = acc_ref[mp] * alpha + jnp.dot(vt, p.astype(BF16), preferred_element_type=F32)
            out.append((m_new, l_new))
        return tuple(out)

    def pair(jj, carry):
        j0 = 2 * jj
        qk(j0 + 1, sb_ref)
        carry = soft_pv(j0, sa_ref, carry, False)
        qk(j0 + 2, sa_ref)
        return soft_pv(j0 + 1, sb_ref, carry, False)

    def even_tail(carry):
        return soft_pv(i, sa_ref, carry, True)

    def odd_tail(carry):
        qk(i, sb_ref)
        carry = soft_pv(i - 1, sa_ref, carry, False)
        return soft_pv(i, sb_ref, carry, True)

    qk(0, sa_ref)
    init = tuple((jnp.full((1, t), NEG_BIG, F32), jnp.zeros((1, t), F32)) for _ in range(2))
    (_, l0), (_, l1) = lax.cond(i % 2 == 0, even_tail, odd_tail, lax.fori_loop(0, i // 2, pair, init))
    o = (acc_ref[0] / l0 - lam * (acc_ref[1] / l1)).T
    var = jnp.mean(o * o, axis=-1, keepdims=True)
    y = o * lax.rsqrt(var + RMS_EPS) * ng_ref[...] * (1.0 - lam_init)
    g = g_ref[...].astype(F32)
    o_ref[...] = (y * _silu(g)).astype(o_ref.dtype)


def _diff(hb, ht, lam_p, ng, lam_init, t):
    s = hb.shape[0]
    w = DIFF_DV
    return pl.pallas_call(
        functools.partial(_diff_kernel, t=t, lam_init=lam_init),
        out_shape=jax.ShapeDtypeStruct((s, DIFF_HEADS * DIFF_DV), BF16),
        grid=(DIFF_HEADS, s // t),
        in_specs=[
            pl.BlockSpec((4, DIFF_DQK), lambda h, i: (0, 0)),
            pl.BlockSpec((w, t), lambda h, i: (HT_BQ // w + h, i)),
            pl.BlockSpec((s, w), lambda h, i: (0, _HB_OFF["bk"] // w + h)),
            pl.BlockSpec((w, s), lambda h, i: (HT_BV // w + h, 0)),
            pl.BlockSpec((t, w), lambda h, i: (i, _HB_OFF["bg"] // w + h)),
            pl.BlockSpec((1, DIFF_DV), lambda h, i: (0, 0)),
        ],
        out_specs=pl.BlockSpec((t, w), lambda h, i: (i, h)),
        scratch_shapes=[pltpu.VMEM((2, DIFF_DV, t), F32), pltpu.VMEM((2, t, t), F32), pltpu.VMEM((2, t, t), F32)],
        compiler_params=_cparams(("parallel", "arbitrary")),
        name="diff_attn",
    )(lam_p, ht, hb, ht, hb, ng)


def _spa_norm_kernel(ckv_ref, sm_ref, kvg_ref, lg_ref, lb_ref, cn_ref, cnt_ref, ikn_ref):
    c = ckv_ref[...]
    cn = c * lax.rsqrt(jnp.mean(c * c, axis=-1, keepdims=True) + RMS_EPS) * kvg_ref[...]
    cn_ref[...] = cn.astype(cn_ref.dtype)
    cnt_ref[0:SPA_LAT, :] = cn.T.astype(cnt_ref.dtype)
    pad_rows = cnt_ref.shape[0] - SPA_LAT
    ones_row = lax.broadcasted_iota(I32, (pad_rows, cnt_ref.shape[1]), 0) == 0
    cnt_ref[SPA_LAT:, :] = jnp.where(ones_row, 1.0, 0.0).astype(cnt_ref.dtype)
    ik = sm_ref[:, SM_CIK:SM_CIK + IDX_DIM]
    mu = jnp.mean(ik, axis=-1, keepdims=True)
    d = ik - mu
    var = jnp.mean(d * d, axis=-1, keepdims=True)
    ikn_ref[...] = (d * lax.rsqrt(var + LN_EPS) * lg_ref[...] + lb_ref[...]).astype(ikn_ref.dtype)


def _spa_norm(hf, kvg, lg, lb, tk):
    s = hf.shape[0]
    nt = s // tk
    return pl.pallas_call(
        _spa_norm_kernel,
        out_shape=(jax.ShapeDtypeStruct((nt, tk, SPA_LAT), BF16),
                   jax.ShapeDtypeStruct((nt, SPA_LATP, tk), BF16),
                   jax.ShapeDtypeStruct((nt, tk, IDX_DIM), BF16)),
        grid=(nt,),
        in_specs=[
            pl.BlockSpec((tk, SPA_LAT), lambda i: (i, HF_CKV // SPA_LAT)),
            pl.BlockSpec((tk, 128), lambda i: (i, HF_SMALL // 128)),
            pl.BlockSpec((1, SPA_LAT), lambda i: (0, 0)),
            pl.BlockSpec((1, IDX_DIM), lambda i: (0, 0)),
            pl.BlockSpec((1, IDX_DIM), lambda i: (0, 0)),
        ],
        out_specs=(pl.BlockSpec((None, tk, SPA_LAT), lambda i: (i, 0, 0)),
                   pl.BlockSpec((None, SPA_LATP, tk), lambda i: (i, 0, 0)),
                   pl.BlockSpec((None, tk, IDX_DIM), lambda i: (i, 0, 0))),
        compiler_params=_cparams(("parallel",)),
        name="spa_norm",
    )(hf, hf, kvg, lg, lb)


def _spa_kernel(iq_ref, q8_ref, w_ref, g_ref, ikn_ref, cn_ref, cnt_ref, wuv_ref, o_ref, keys_ref, acc_ref,
                iqt_ref, q8t_ref, sa_ref, sb_ref, top_ref, *, tq, tk, topk, idx_bits):
    i = pl.program_id(0)
    for h in range(IDX_HEADS):
        iqt_ref[:, h * tq:(h + 1) * tq] = iq_ref[h * IDX_DIM:(h + 1) * IDX_DIM, :]
    for h in range(SPA_HEADS):
        q8t_ref[:, h * tq:(h + 1) * tq] = q8_ref[h * SPA_LAT:(h + 1) * SPA_LAT, :]
    n_t = (i * tq + tq + tk - 1) // tk
    qpos = i * tq + lax.broadcasted_iota(I32, (1, tq), 1)
    qch = qpos // CHUNK
    k_row = jnp.minimum(topk, (qch + 1) * CHUNK)
    hk = tk // 2

    def score_tile(j, masked):
        for half in range(2):
            ks = slice(half * hk, (half + 1) * hk)
            ikt = ikn_ref[j, ks, :]
            sc = jnp.zeros((hk, tq), F32)
            for hp in range(IDX_HEADS // 2):
                r = jnp.dot(ikt, iqt_ref[:, hp * 2 * tq:(hp + 1) * 2 * tq], preferred_element_type=F32)
                for hh in range(2):
                    h = hp * 2 + hh
                    sc = sc + jnp.maximum(r[:, hh * tq:(hh + 1) * tq], 0.0) * w_ref[h:h + 1, :]
            bits = pltpu.bitcast(sc, I32)
            key = bits ^ ((bits >> 31) & 0x7FFFFFFF)
            top = pltpu.bitcast(bits & TOP16_MASK, F32).astype(BF16)
            if masked:
                kch = (j * tk + half * hk + lax.broadcasted_iota(I32, (hk, tq), 0)) // CHUNK
                adm = kch <= qch
                key = jnp.where(adm, key, INT_MIN)
                top = jnp.where(adm, top, -jnp.inf).astype(BF16)
            keys_ref[j, ks, :] = key
            top_ref[j, ks, :] = top

    def body_a(j, _):
        score_tile(j, False)
        return 0

    lax.fori_loop(0, n_t - 1, body_a, 0)
    score_tile(n_t - 1, True)

    keys_ref[n_t] = jnp.full((tk, tq), INT_MIN, I32)
    n_t2 = (n_t + 1) // 2

    def count_where(pred):
        def body(jj, cnt):
            for j in (2 * jj, 2 * jj + 1):
                ind = jnp.where(pred(keys_ref[j], j), 1, 0).astype(I32)
                cnt = cnt + jnp.sum(ind.reshape(tk // 8, 8, tq), axis=0)
            return cnt
        cnt = lax.fori_loop(0, n_t2, body, jnp.zeros((8, tq), I32))
        return jnp.sum(cnt, axis=0, keepdims=True)

    def bis_cond(st):
        b, _, cnt_a = st
        return jnp.logical_and(b < 32, jnp.max(cnt_a - k_row) > 0)

    def bis_body(st):
        b, a, cnt_a = st
        cand_a = a | lax.shift_left(jnp.int32(1), 31 - b)
        cand = cand_a ^ INT_MIN
        cnt = count_where(lambda kt, j: kt >= cand)
        take = cnt >= k_row
        return b + 1, jnp.where(take, cand_a, a), jnp.where(take, cnt, cnt_a)

    top_ref[n_t] = jnp.full((tk, tq), -jnp.inf, BF16)
    k_row_f = k_row.astype(F32)

    def count_top(cand):
        def body(jj, cnt):
            for j in (2 * jj, 2 * jj + 1):
                ind = jnp.where(top_ref[j] >= cand, one_bf, zero_bf).reshape(tk // 16, 16, tq)
                parts = [ind[r] for r in range(tk // 16)]
                while len(parts) > 1:
                    parts = [parts[r] + parts[r + 1] for r in range(0, len(parts), 2)]
                cnt = cnt + parts[0]
            return cnt
        cnt = lax.fori_loop(0, n_t2, body, jnp.zeros((16, tq), BF16))
        return jnp.sum(cnt.astype(F32), axis=0, keepdims=True)

    one_bf = jnp.ones((tk, tq), BF16)
    zero_bf = jnp.zeros((tk, tq), BF16)

    def top_body(b, a16):
        cand_a = a16 | lax.shift_left(jnp.int32(1), 15 - b)
        k16 = cand_a ^ 0x8000
        neg = (k16 >> 15) & 1
        fbits = k16 ^ (neg * 0x7FFF)
        mag = fbits & 0x7FFF
        subnormal = jnp.logical_and(mag > 0, mag < 0x80)
        fbits = jnp.where(subnormal, jnp.where(neg == 1, 0, 0x80), fbits)
        cand = pltpu.bitcast(lax.shift_left(fbits, 16), F32).astype(BF16)
        return jnp.where(count_top(cand) >= k_row_f, cand_a, a16)

    a16 = lax.fori_loop(0, 16, top_body, jnp.zeros((1, tq), I32))
    st0 = (jnp.int32(16), lax.shift_left(a16, 16), jnp.zeros((1, tq), I32) + n_t * tk)
    st1 = lax.fori_loop(16, BISECT_BLIND_BITS, lambda _, st: bis_body(st), st0)
    _, a_fin, cnt_fin = lax.while_loop(bis_cond, bis_body, st1)
    thr = a_fin ^ INT_MIN

    @pl.when(jnp.max(cnt_fin - k_row) > 0)
    def _():
        need = k_row - count_where(lambda kt, j: kt > thr)

        def idx_of(j):
            return j * tk + lax.broadcasted_iota(I32, (tk, tq), 0)

        def idx_body(b, y):
            cand = y | lax.shift_left(jnp.int32(1), idx_bits - 1 - b)
            cnt = count_where(lambda kt, j: jnp.logical_and(kt == thr, idx_of(j) < cand))
            return jnp.where(cnt < need, cand, y)

        y_keep = lax.fori_loop(0, idx_bits, idx_body, jnp.zeros((1, tq), I32))

        def fix_body(j, _):
            kt = keys_ref[j]
            drop = jnp.logical_and(kt == thr, idx_of(j) > y_keep)
            keys_ref[j] = jnp.where(drop, kt - 1, kt)
            return 0

        lax.fori_loop(0, n_t, fix_body, 0)

    acc_ref[...] = jnp.zeros_like(acc_ref)

    last = keys_ref.shape[0] - 2

    def qk(j, s_ref):
        s_ref[...] = jnp.dot(cn_ref[jnp.minimum(j, last)], q8t_ref[...], preferred_element_type=F32)

    def soft_pv(j, s_ref, m_old):
        madd = jnp.where(keys_ref[j] >= thr, 0.0, NEG_BIG)
        s = s_ref[...] + jnp.concatenate([madd] * SPA_HEADS, axis=1)
        m_new = jnp.maximum(m_old, jnp.max(s, axis=0, keepdims=True))
        p = jnp.exp2(s - m_new)
        alpha = jnp.exp2(m_old - m_new)
        acc_ref[...] = acc_ref[...] * alpha + jnp.dot(cnt_ref[jnp.minimum(j, last)], p.astype(BF16),
                                                      preferred_element_type=F32)
        return m_new

    def body_c(jj, m_run):
        j0 = 2 * jj
        qk(j0 + 1, sb_ref)
        m_run = soft_pv(j0, sa_ref, m_run)
        qk(j0 + 2, sa_ref)
        return soft_pv(j0 + 1, sb_ref, m_run)

    qk(0, sa_ref)
    lax.fori_loop(0, n_t2, body_c, jnp.full((1, SPA_HEADS * tq), NEG_BIG, F32))
    l_fin = acc_ref[SPA_LAT:SPA_LAT + 1, :]

    for h in range(SPA_HEADS):
        cs = slice(h * tq, (h + 1) * tq)
        o_lat = (acc_ref[0:SPA_LAT, cs] / l_fin[:, cs]).T
        oh = jnp.dot(o_lat.astype(BF16), wuv_ref[h], preferred_element_type=F32)
        g = g_ref[:, h * SPA_DV:(h + 1) * SPA_DV].astype(F32)
        o_ref[:, h * SPA_DV:(h + 1) * SPA_DV] = (oh * _silu(g)).astype(o_ref.dtype)


def _spa(hb, ht, wt, ikn, cn, cnt, wuv, tq, tk, topk):
    s = hb.shape[0]
    nq, nt = s // tq, s // tk
    const3 = lambda i: (0, 0, 0)
    return pl.pallas_call(
        functools.partial(_spa_kernel, tq=tq, tk=tk, topk=topk, idx_bits=(s - 1).bit_length()),
        out_shape=jax.ShapeDtypeStruct((s, SPA_HEADS * SPA_DV), BF16),
        grid=(nq,),
        in_specs=[
            pl.BlockSpec((IDX_HEADS * IDX_DIM, tq), lambda i: (HT_CIQ // (IDX_HEADS * IDX_DIM), i)),
            pl.BlockSpec((SPA_HEADS * SPA_LAT, tq), lambda i: (HT_CQ // (SPA_HEADS * SPA_LAT), i)),
            pl.BlockSpec((None, IDX_HEADS, tq), lambda i: (i, 0, 0)),
            pl.BlockSpec((tq, 1024), lambda i: (i, _HB_OFF["cg"] // 1024)),
            pl.BlockSpec((nt, tk, IDX_DIM), const3, pipeline_mode=pl.Buffered(1)),
            pl.BlockSpec((nt, tk, SPA_LAT), const3, pipeline_mode=pl.Buffered(1)),
            pl.BlockSpec((nt, SPA_LATP, tk), const3, pipeline_mode=pl.Buffered(1)),
            pl.BlockSpec((SPA_HEADS, SPA_LAT, SPA_DV), const3, pipeline_mode=pl.Buffered(1)),
        ],
        out_specs=pl.BlockSpec((tq, 1024), lambda i: (i, 0)),
        scratch_shapes=[pltpu.VMEM((nt + 1, tk, tq), I32), pltpu.VMEM((SPA_LATP, SPA_HEADS * tq), F32),
                        pltpu.VMEM((IDX_DIM, IDX_HEADS * tq), BF16), pltpu.VMEM((SPA_LAT, SPA_HEADS * tq), BF16),
                        pltpu.VMEM((tk, SPA_HEADS * tq), F32), pltpu.VMEM((tk, SPA_HEADS * tq), F32),
                        pltpu.VMEM((nt + 1, tk, tq), BF16)],
        compiler_params=_cparams(("arbitrary",)),
        name="sparse_attn",
    )(ht, ht, wt, hb, ikn, cn, cnt, wuv)


def _out_kernel(oa_ref, ob_ref, oc_ref, x_ref, w_ref, lg_ref, lb_ref, y_ref, yb_ref, *, alpha):
    d = y_ref.shape[1]
    ncol = d // OUT_NC
    na, nb = oa_ref.shape[1], ob_ref.shape[1]
    oa, ob, oc = oa_ref[...], ob_ref[...], oc_ref[...]
    ssum = 0.0
    for c in range(ncol):
        cs = slice(c * OUT_NC, (c + 1) * OUT_NC)
        r = (alpha * x_ref[:, cs]
             + jnp.dot(oa, w_ref[0:na, cs], preferred_element_type=F32)
             + jnp.dot(ob, w_ref[na:na + nb, cs], preferred_element_type=F32)
             + jnp.dot(oc, w_ref[na + nb:, cs], preferred_element_type=F32))
        y_ref[:, cs] = r
        ssum = ssum + jnp.sum(r, axis=-1, keepdims=True)
    mu = ssum / d
    vsum = 0.0
    for c in range(ncol):
        cen = y_ref[:, c * OUT_NC:(c + 1) * OUT_NC] - mu
        vsum = vsum + jnp.sum(cen * cen, axis=-1, keepdims=True)
    rstd = lax.rsqrt(vsum / d + LN_EPS)
    for c in range(ncol):
        cs = slice(c * OUT_NC, (c + 1) * OUT_NC)
        y = (y_ref[:, cs] - mu) * rstd * lg_ref[:, cs] + lb_ref[:, cs]
        y_ref[:, cs] = y
        yb_ref[:, cs] = y.astype(BF16)


def _out_proj(oa, ob, oc, x, w, lg, lb, alpha, tm):
    s, d = x.shape
    row = lambda i: (i, 0)
    const = lambda i: (0, 0)
    return pl.pallas_call(
        functools.partial(_out_kernel, alpha=alpha),
        out_shape=(jax.ShapeDtypeStruct((s, d), F32), jax.ShapeDtypeStruct((s, d), BF16)),
        grid=(s // tm,),
        in_specs=[
            pl.BlockSpec((tm, oa.shape[1]), row),
            pl.BlockSpec((tm, ob.shape[1]), row),
            pl.BlockSpec((tm, oc.shape[1]), row),
            pl.BlockSpec((tm, d), row),
            pl.BlockSpec(w.shape, const, pipeline_mode=pl.Buffered(1)),
            pl.BlockSpec((1, d), const),
            pl.BlockSpec((1, d), const),
        ],
        out_specs=(pl.BlockSpec((tm, d), row), pl.BlockSpec((tm, d), row)),
        compiler_params=_cparams(("parallel",)),
        name="out_proj_ln",
    )(oa, ob, oc, x, w, lg, lb)


def kernel(x, w_in, w_out, gla_w_gate2, gla_b_gate, gla_norm_g, diff_lambda, diff_norm_g,
           spa_kv_norm_g, spa_ik_ln_g, spa_ik_ln_b, spa_w_uv, post_ln_g, post_ln_b):
    bsz, s, d = x.shape
    assert bsz == 1 and d == 4096 and w_in.shape[2] == sum(_IN_WIDTHS)
    depth = w_in.shape[0]
    alpha = (2.0 * depth) ** 0.25
    topk = min(IDX_TOPK_MAX, s // 4)
    tm_in = min(1024, s)
    t_diff = min(512, s)
    tq_s, tk_s = 128, min(512, s)
    nq_s = s // tq_s

    xf = x[0]
    xb = xf.astype(BF16)
    for l in range(depth):
        wb, wf, wqt = _prep_w_in(w_in[l])
        tn = IN_TN
        hb = _matmul(xb, wb, BF16, tm_in, tn)
        hf = _matmul(xb, wf, F32, tm_in, tn)
        ht = _matmul_t(wqt, xb, BF16, tn, tm_in,
                       fold=((HT_CQ // tn, (HT_CQ + _IN_OFF["cq"][1]) // tn, _Q_FOLD["cq"]),
                             (HT_BQ // tn, (HT_BQ + _IN_OFF["bq"][1]) // tn, _Q_FOLD["bq"])))

        o_a = _gla(hb, hf, gla_w_gate2[l], gla_b_gate[l][None], gla_norm_g[l][None], min(256, s))

        lam_init = 0.8 - 0.6 * math.exp(-0.3 * l)
        o_b = _diff(hb, ht, diff_lambda[l], diff_norm_g[l][None], lam_init, t_diff)

        cn, cnt, ikn = _spa_norm(hf, spa_kv_norm_g[l][None], spa_ik_ln_g[l][None], spa_ik_ln_b[l][None], tk_s)
        ciw = lax.slice_in_dim(hf, HF_SMALL + SM_CIW, HF_SMALL + SM_CIW + IDX_HEADS, axis=1)
        wt = (ciw * (IDX_HEADS ** -0.5 * IDX_DIM ** -0.5)).reshape(nq_s, tq_s, IDX_HEADS).transpose(0, 2, 1)
        o_c = _spa(hb, ht, wt, ikn, cn, cnt, spa_w_uv[l].astype(BF16), tq_s, tk_s, topk)

        xf, xb = _out_proj(o_a, o_b, o_c, xf, w_out[l].astype(BF16), post_ln_g[l][None], post_ln_b[l][None],
                           alpha, min(128, s))
    return xf[None]
```

```python
import functools
import math

import jax
import jax.numpy as jnp
from jax import lax
from jax.experimental import pallas as pl
from jax.experimental.pallas import tpu as pltpu

F32 = jnp.float32
BF16 = jnp.bfloat16
I32 = jnp.int32

CHUNK = 64
GLA_HEADS, GLA_DK, GLA_DV, GLA_RANK, GLA_TAU = 4, 256, 512, 16, 16.0
DIFF_HEADS, DIFF_DQK, DIFF_DV = 4, 128, 256
SPA_HEADS, SPA_DV, SPA_LAT = 8, 128, 256
IDX_HEADS, IDX_DIM, IDX_TOPK_MAX = 16, 64, 256
SPA_LATP = SPA_LAT + 16
LN_EPS, RMS_EPS = 1e-5, 1e-6

_IN_WIDTHS = (1024, 1024, 2048, 16, 2048, 1024, 1024, 1024, 1024, 2048, 256, 1024, 64, 16, 1024)
_IN_NAMES = ("aq", "ak", "av", "aa", "ag", "bq", "bk", "bv", "bg", "cq", "ckv", "ciq", "cik", "ciw", "cg")
_IN_OFF = {}
_acc = 0
for _n, _w in zip(_IN_NAMES, _IN_WIDTHS):
    _IN_OFF[_n] = (_acc, _w)
    _acc += _w

_HB_ORDER = ("av", "ag", "aq", "bk", "bg", "cg")
_HB_OFF = {}
_acc = 0
for _n in _HB_ORDER:
    _HB_OFF[_n] = _acc
    _acc += _IN_OFF[_n][1]
HB_WIDTH = _acc
_HT_ORDER = ("cq", "ciq", "bq", "bv")
HT_CQ, HT_CIQ, HT_BQ, HT_BV, HT_WIDTH = 0, 2048, 3072, 4096, 5120
HF_AK, HF_CKV, HF_SMALL, HF_WIDTH = 0, 1024, 1280, 1536
SM_CIK, SM_AA, SM_CIW = 0, 64, 80

_Q_FOLD = {"bq": DIFF_DQK ** -0.5 * math.log2(math.e), "cq": SPA_LAT ** -0.5 * math.log2(math.e)}
IN_TN = 512
COUNT_ROWS = 512
BISECT_BLIND_BITS = 24
OUT_NC = 1024
VMEM_LIMIT = 56 * 1024 * 1024
NEG_BIG = -1e30
INT_MIN = -(2 ** 31)


def _cparams(sem):
    return pltpu.CompilerParams(dimension_semantics=sem, vmem_limit_bytes=VMEM_LIMIT)


def _silu(g):
    return g / (1.0 + jnp.exp(-g))


def _tile_scale(folds):
    j = pl.program_id(1)
    scale = jnp.float32(1.0)
    for lo, hi, c in folds:
        scale = jnp.where(jnp.logical_and(j >= lo, j < hi), c, scale)
    return scale


def _matmul_kernel(x_ref, w_ref, o_ref, *, fold):
    acc = jnp.dot(x_ref[...], w_ref[...], preferred_element_type=F32)
    o_ref[...] = (acc * _tile_scale(fold)).astype(o_ref.dtype)


def _matmul(x, w, out_dtype, tm, tn, fold=()):
    m, k = x.shape
    n = w.shape[1]
    return pl.pallas_call(
        functools.partial(_matmul_kernel, fold=fold),
        out_shape=jax.ShapeDtypeStruct((m, n), out_dtype),
        grid=(m // tm, n // tn),
        in_specs=[pl.BlockSpec((tm, k), lambda i, j: (i, 0)),
                  pl.BlockSpec((k, tn), lambda i, j: (0, j))],
        out_specs=pl.BlockSpec((tm, tn), lambda i, j: (i, j)),
        compiler_params=_cparams(("parallel", "parallel")),
        name="in_proj",
    )(x, w)


def _matmul_t_kernel(wt_ref, x_ref, o_ref, *, fold):
    acc = lax.dot_general(wt_ref[...], x_ref[...], (((1,), (1,)), ((), ())), preferred_element_type=F32)
    o_ref[...] = (acc * _tile_scale(fold)).astype(o_ref.dtype)


def _matmul_t(wt, x, out_dtype, tn, tm, fold=()):
    n, k = wt.shape
    m = x.shape[0]
    return pl.pallas_call(
        functools.partial(_matmul_t_kernel, fold=fold),
        out_shape=jax.ShapeDtypeStruct((n, m), out_dtype),
        grid=(m // tm, n // tn),
        in_specs=[pl.BlockSpec((tn, k), lambda i, j: (j, 0)),
                  pl.BlockSpec((tm, k), lambda i, j: (i, 0))],
        out_specs=pl.BlockSpec((tn, tm), lambda i, j: (j, i)),
        compiler_params=_cparams(("parallel", "parallel")),
        name="in_proj_t",
    )(wt, x)


def _prep_w_in(w):
    def cols(name):
        o, n = _IN_OFF[name]
        return w[:, o:o + n]
    d = w.shape[0]
    wb = jnp.concatenate([cols(n) for n in _HB_ORDER], axis=1).astype(BF16)
    small = jnp.concatenate([cols("cik"), cols("aa"), cols("ciw"), jnp.zeros((d, 32), w.dtype)], axis=1)
    wf = jnp.concatenate([cols("ak"), cols("ckv"), small, jnp.zeros((d, HF_WIDTH - HF_SMALL - 128), w.dtype)],
                         axis=1).astype(BF16)
    wt = jnp.concatenate([cols(n) for n in _HT_ORDER], axis=1).astype(BF16).T
    return wb, wf, wt


def _gla_kernel(v_ref, q_ref, g_ref, k_ref, sm_ref, w2_ref, b_ref, ng_ref, o_ref, st_ref, *, nchunk):
    @pl.when(pl.program_id(0) == 0)
    def _():
        st_ref[...] = jnp.zeros_like(st_ref)

    row = lax.broadcasted_iota(I32, (CHUNK, CHUNK), 0)
    col = lax.broadcasted_iota(I32, (CHUNK, CHUNK), 1)
    tri = (col <= row).astype(F32)
    chunks = [slice(c * CHUNK, (c + 1) * CHUNK) for c in range(nchunk)]
    kss = [slice(h * GLA_DK, (h + 1) * GLA_DK) for h in range(GLA_HEADS)]
    vss = [slice(h * GLA_DV, (h + 1) * GLA_DV) for h in range(GLA_HEADS)]

    aa = sm_ref[:, SM_AA:SM_AA + GLA_RANK].astype(BF16)
    z = jnp.dot(aa, w2_ref[...].astype(BF16), preferred_element_type=F32) + b_ref[...]
    log_a = (jnp.minimum(z, 0.0) - jnp.log1p(jnp.exp(-jnp.abs(z)))) * (1.0 / GLA_TAU)
    cums = [jnp.dot(tri, log_a[rs, :], preferred_element_type=F32, precision=lax.Precision.HIGHEST)
            for rs in chunks]
    tots = [cum[CHUNK - 1:CHUNK, :] for cum in cums]
    k_decs = [(k_ref[rs, :] * jnp.exp(tot - cum)).astype(BF16) for rs, cum, tot in zip(chunks, cums, tots)]
    decays = [jnp.exp(tot) for tot in tots]
    upds = [[lax.dot_general(v_ref[rs, vss[h]], k_dec[:, kss[h]], (((0,), (0,)), ((), ())),
                             preferred_element_type=F32) for h in range(GLA_HEADS)]
            for rs, k_dec in zip(chunks, k_decs)]
    states = [[None] * GLA_HEADS for _ in range(nchunk)]
    for h in range(GLA_HEADS):
        st = st_ref[h]
        for c in range(nchunk):
            st = st * decays[c][:, kss[h]] + upds[c][h]
            states[c][h] = st.astype(BF16)
        st_ref[h] = st
    outs = [[lax.dot_general(q_ref[rs, kss[h]], states[c][h], (((1,), (1,)), ((), ())),
                             preferred_element_type=F32) for h in range(GLA_HEADS)]
            for c, rs in enumerate(chunks)]
    for c, rs in enumerate(chunks):
        for h in range(GLA_HEADS):
            o = outs[c][h] * (GLA_DK ** -0.5)
            var = jnp.mean(o * o, axis=-1, keepdims=True)
            y = o * lax.rsqrt(var + RMS_EPS) * ng_ref[...]
            g = g_ref[rs, vss[h]].astype(F32)
            o_ref[rs, vss[h]] = (y * _silu(g)).astype(o_ref.dtype)


def _gla(hb, hf, w2, b, ng, tr):
    s = hb.shape[0]
    nchunk = tr // CHUNK
    return pl.pallas_call(
        functools.partial(_gla_kernel, nchunk=nchunk),
        out_shape=jax.ShapeDtypeStruct((s, GLA_HEADS * GLA_DV), BF16),
        grid=(s // tr,),
        in_specs=[
            pl.BlockSpec((tr, 2048), lambda i: (i, _HB_OFF["av"] // 2048)),
            pl.BlockSpec((tr, 1024), lambda i: (i, _HB_OFF["aq"] // 1024)),
            pl.BlockSpec((tr, 2048), lambda i: (i, _HB_OFF["ag"] // 2048)),
            pl.BlockSpec((tr, 1024), lambda i: (i, HF_AK // 1024)),
            pl.BlockSpec((tr, 128), lambda i: (i, HF_SMALL // 128)),
            pl.BlockSpec((GLA_RANK, 1024), lambda i: (0, 0)),
            pl.BlockSpec((1, 1024), lambda i: (0, 0)),
            pl.BlockSpec((1, GLA_DV), lambda i: (0, 0)),
        ],
        out_specs=pl.BlockSpec((tr, 2048), lambda i: (i, 0)),
        scratch_shapes=[pltpu.VMEM((GLA_HEADS, GLA_DV, GLA_DK), F32)],
        compiler_params=_cparams(("arbitrary",)),
        name="gla",
    )(hb, hb, hb, hf, hf, w2, b, ng)


def _diff_kernel(lam_ref, q_ref, k_ref, v_ref, g_ref, ng_ref, o_ref, acc_ref, sa_ref, sb_ref, *, t, lam_init):
    i = pl.program_id(1)
    lp = lam_ref[...]
    lam = (jnp.exp(jnp.sum(lp[0:1] * lp[1:2], axis=(0, 1), keepdims=True))
           - jnp.exp(jnp.sum(lp[2:3] * lp[3:4], axis=(0, 1), keepdims=True)) + lam_init)
    acc_ref[...] = jnp.zeros_like(acc_ref)
    maps = [slice(mp * DIFF_DQK, (mp + 1) * DIFF_DQK) for mp in range(2)]

    def qk(j, s_ref):
        kt = k_ref[pl.ds(pl.multiple_of(j * t, t), t), :]
        for mp in range(2):
            s_ref[mp] = jnp.dot(kt[:, maps[mp]], q_ref[maps[mp], :], preferred_element_type=F32)

    def soft_pv(j, s_ref, carry, masked):
        vt = v_ref[:, pl.ds(pl.multiple_of(j * t, t), t)]
        out = []
        for mp in range(2):
            m_old, l_old = carry[mp]
            s = s_ref[mp]
            if masked:
                kch = lax.broadcasted_iota(I32, (t, t), 0) // CHUNK
                qch = lax.broadcasted_iota(I32, (t, t), 1) // CHUNK
                s = jnp.where(kch <= qch, s, NEG_BIG)
            m_new = jnp.maximum(m_old, jnp.max(s, axis=0, keepdims=True))
            p = jnp.exp2(s - m_new)
            alpha = jnp.exp2(m_old - m_new)
            l_new = alpha * l_old + jnp.sum(p, axis=0, keepdims=True)
            acc_ref[mp] = acc_ref[mp] * alpha + jnp.dot(vt, p.astype(BF16), preferred_element_type=F32)
            out.append((m_new, l_new))
        return tuple(out)

    def pair(jj, carry):
        j0 = 2 * jj
        qk(j0 + 1, sb_ref)
        carry = soft_pv(j0, sa_ref, carry, False)
        qk(j0 + 2, sa_ref)
        return soft_pv(j0 + 1, sb_ref, carry, False)

    def even_tail(carry):
        return soft_pv(i, sa_ref, carry, True)

    def odd_tail(carry):
        qk(i, sb_ref)
        carry = soft_pv(i - 1, sa_ref, carry, False)
        return soft_pv(i, sb_ref, carry, True)

    qk(0, sa_ref)
    init = tuple((jnp.full((1, t), NEG_BIG, F32), jnp.zeros((1, t), F32)) for _ in range(2))
    (_, l0), (_, l1) = lax.cond(i % 2 == 0, even_tail, odd_tail, lax.fori_loop(0, i // 2, pair, init))
    o = (acc_ref[0] / l0 - lam * (acc_ref[1] / l1)).T
    var = jnp.mean(o * o, axis=-1, keepdims=True)
    y = o * lax.rsqrt(var + RMS_EPS) * ng_ref[...] * (1.0 - lam_init)
    g = g_ref[...].astype(F32)
    o_ref[...] = (y * _silu(g)).astype(o_ref.dtype)


def _diff(hb, ht, lam_p, ng, lam_init, t):
    s = hb.shape[0]
    w = DIFF_DV
    return pl.pallas_call(
        functools.partial(_diff_kernel, t=t, lam_init=lam_init),
        out_shape=jax.ShapeDtypeStruct((s, DIFF_HEADS * DIFF_DV), BF16),
        grid=(DIFF_HEADS, s // t),
        in_specs=[
            pl.BlockSpec((4, DIFF_DQK), lambda h, i: (0, 0)),
            pl.BlockSpec((w, t), lambda h, i: (HT_BQ // w + h, i)),
            pl.BlockSpec((s, w), lambda h, i: (0, _HB_OFF["bk"] // w + h)),
            pl.BlockSpec((w, s), lambda h, i: (HT_BV // w + h, 0)),
            pl.BlockSpec((t, w), lambda h, i: (i, _HB_OFF["bg"] // w + h)),
            pl.BlockSpec((1, DIFF_DV), lambda h, i: (0, 0)),
        ],
        out_specs=pl.BlockSpec((t, w), lambda h, i: (i, h)),
        scratch_shapes=[pltpu.VMEM((2, DIFF_DV, t), F32), pltpu.VMEM((2, t, t), F32), pltpu.VMEM((2, t, t), F32)],
        compiler_params=_cparams(("parallel", "arbitrary")),
        name="diff_attn",
    )(lam_p, ht, hb, ht, hb, ng)


def _spa_norm_kernel(ckv_ref, sm_ref, kvg_ref, lg_ref, lb_ref, cn_ref, cnt_ref, ikn_ref):
    c = ckv_ref[...]
    cn = c * lax.rsqrt(jnp.mean(c * c, axis=-1, keepdims=True) + RMS_EPS) * kvg_ref[...]
    cn_ref[...] = cn.astype(cn_ref.dtype)
    cnt_ref[0:SPA_LAT, :] = cn.T.astype(cnt_ref.dtype)
    pad_rows = cnt_ref.shape[0] - SPA_LAT
    ones_row = lax.broadcasted_iota(I32, (pad_rows, cnt_ref.shape[1]), 0) == 0
    cnt_ref[SPA_LAT:, :] = jnp.where(ones_row, 1.0, 0.0).astype(cnt_ref.dtype)
    ik = sm_ref[:, SM_CIK:SM_CIK + IDX_DIM]
    mu = jnp.mean(ik, axis=-1, keepdims=True)
    d = ik - mu
    var = jnp.mean(d * d, axis=-1, keepdims=True)
    ikn_ref[...] = (d * lax.rsqrt(var + LN_EPS) * lg_ref[...] + lb_ref[...]).astype(ikn_ref.dtype)


def _spa_norm(hf, kvg, lg, lb, tk):
    s = hf.shape[0]
    nt = s // tk
    return pl.pallas_call(
        _spa_norm_kernel,
        out_shape=(jax.ShapeDtypeStruct((nt, tk, SPA_LAT), BF16),
                   jax.ShapeDtypeStruct((nt, SPA_LATP, tk), BF16),
                   jax.ShapeDtypeStruct((nt, tk, IDX_DIM), BF16)),
        grid=(nt,),
        in_specs=[
            pl.BlockSpec((tk, SPA_LAT), lambda i: (i, HF_CKV // SPA_LAT)),
            pl.BlockSpec((tk, 128), lambda i: (i, HF_SMALL // 128)),
            pl.BlockSpec((1, SPA_LAT), lambda i: (0, 0)),
            pl.BlockSpec((1, IDX_DIM), lambda i: (0, 0)),
            pl.BlockSpec((1, IDX_DIM), lambda i: (0, 0)),
        ],
        out_specs=(pl.BlockSpec((None, tk, SPA_LAT), lambda i: (i, 0, 0)),
                   pl.BlockSpec((None, SPA_LATP, tk), lambda i: (i, 0, 0)),
                   pl.BlockSpec((None, tk, IDX_DIM), lambda i: (i, 0, 0))),
        compiler_params=_cparams(("parallel",)),
        name="spa_norm",
    )(hf, hf, kvg, lg, lb)


def _spa_kernel(iq_ref, q8_ref, w_ref, g_ref, ikn_ref, cn_ref, cnt_ref, wuv_ref, o_ref, keys_ref, acc_ref,
                iqt_ref, q8t_ref, sa_ref, sb_ref, *, tq, tk, topk, idx_bits):
    i = pl.program_id(0)
    for h in range(IDX_HEADS):
        iqt_ref[:, h * tq:(h + 1) * tq] = iq_ref[h * IDX_DIM:(h + 1) * IDX_DIM, :]
    for h in range(SPA_HEADS):
        q8t_ref[:, h * tq:(h + 1) * tq] = q8_ref[h * SPA_LAT:(h + 1) * SPA_LAT, :]
    n_t = (i * tq + tq + tk - 1) // tk
    qpos = i * tq + lax.broadcasted_iota(I32, (1, tq), 1)
    qch = qpos // CHUNK
    k_row = jnp.minimum(topk, (qch + 1) * CHUNK)
    hk = tk // 2

    def score_tile(j, masked):
        for half in range(2):
            ks = slice(half * hk, (half + 1) * hk)
            ikt = ikn_ref[j, ks, :]
            sc = jnp.zeros((hk, tq), F32)
            for hp in range(IDX_HEADS // 2):
                r = jnp.dot(ikt, iqt_ref[:, hp * 2 * tq:(hp + 1) * 2 * tq], preferred_element_type=F32)
                for hh in range(2):
                    h = hp * 2 + hh
                    sc = sc + jnp.maximum(r[:, hh * tq:(hh + 1) * tq], 0.0) * w_ref[h:h + 1, :]
            bits = pltpu.bitcast(sc, I32)
            key = bits ^ ((bits >> 31) & 0x7FFFFFFF)
            if masked:
                kch = (j * tk + half * hk + lax.broadcasted_iota(I32, (hk, tq), 0)) // CHUNK
                key = jnp.where(kch <= qch, key, INT_MIN)
            keys_ref[j, ks, :] = key

    def body_a(j, _):
        score_tile(j, False)
        return 0

    lax.fori_loop(0, n_t - 1, body_a, 0)
    score_tile(n_t - 1, True)

    keys_ref[n_t] = jnp.full((tk, tq), INT_MIN, I32)
    n_t2 = (n_t + 1) // 2

    cb = min(COUNT_ROWS, tk)

    def count_where(pred):
        def body(jj, cnt):
            for j in (2 * jj, 2 * jj + 1):
                for r0 in range(0, tk, cb):
                    ind = jnp.where(pred(keys_ref[j, r0:r0 + cb, :], j * tk + r0), 1, 0).astype(I32)
                    cnt = cnt + jnp.sum(ind.reshape(cb // 8, 8, tq), axis=0)
            return cnt
        cnt = lax.fori_loop(0, n_t2, body, jnp.zeros((8, tq), I32))
        return jnp.sum(cnt, axis=0, keepdims=True)

    def bis_cond(st):
        b, _, cnt_a = st
        return jnp.logical_and(b < 32, jnp.max(cnt_a - k_row) > 0)

    def bis_body(st):
        b, a, cnt_a = st
        cand_a = a | lax.shift_left(jnp.int32(1), 31 - b)
        cand = cand_a ^ INT_MIN
        cnt = count_where(lambda kt, first: kt >= cand)
        take = cnt >= k_row
        return b + 1, jnp.where(take, cand_a, a), jnp.where(take, cnt, cnt_a)

    st0 = (jnp.int32(0), jnp.zeros((1, tq), I32), jnp.zeros((1, tq), I32) + n_t * tk)
    st1 = lax.fori_loop(0, BISECT_BLIND_BITS, lambda _, st: bis_body(st), st0)
    _, a_fin, cnt_fin = lax.while_loop(bis_cond, bis_body, st1)
    thr = a_fin ^ INT_MIN

    @pl.when(jnp.max(cnt_fin - k_row) > 0)
    def _():
        need = k_row - count_where(lambda kt, first: kt > thr)

        def idx_of(first, rows):
            return first + lax.broadcasted_iota(I32, (rows, tq), 0)

        def idx_body(b, y):
            cand = y | lax.shift_left(jnp.int32(1), idx_bits - 1 - b)
            cnt = count_where(lambda kt, first: jnp.logical_and(kt == thr, idx_of(first, cb) < cand))
            return jnp.where(cnt < need, cand, y)

        y_keep = lax.fori_loop(0, idx_bits, idx_body, jnp.zeros((1, tq), I32))

        def fix_body(j, _):
            kt = keys_ref[j]
            drop = jnp.logical_and(kt == thr, idx_of(j * tk, tk) > y_keep)
            keys_ref[j] = jnp.where(drop, kt - 1, kt)
            return 0

        lax.fori_loop(0, n_t, fix_body, 0)

    acc_ref[...] = jnp.zeros_like(acc_ref)

    last = keys_ref.shape[0] - 2

    def qk(j, s_ref):
        s_ref[...] = jnp.dot(cn_ref[jnp.minimum(j, last)], q8t_ref[...], preferred_element_type=F32)

    def soft_pv(j, s_ref, m_old):
        madd = jnp.where(keys_ref[j] >= thr, 0.0, NEG_BIG)
        s = s_ref[...] + jnp.concatenate([madd] * SPA_HEADS, axis=1)
        m_new = jnp.maximum(m_old, jnp.max(s, axis=0, keepdims=True))
        p = jnp.exp2(s - m_new)
        alpha = jnp.exp2(m_old - m_new)
        acc_ref[...] = acc_ref[...] * alpha + jnp.dot(cnt_ref[jnp.minimum(j, last)], p.astype(BF16),
                                                      preferred_element_type=F32)
        return m_new

    def body_c(jj, m_run):
        j0 = 2 * jj
        qk(j0 + 1, sb_ref)
        m_run = soft_pv(j0, sa_ref, m_run)
        qk(j0 + 2, sa_ref)
        return soft_pv(j0 + 1, sb_ref, m_run)

    qk(0, sa_ref)
    lax.fori_loop(0, n_t2, body_c, jnp.full((1, SPA_HEADS * tq), NEG_BIG, F32))
    l_fin = acc_ref[SPA_LAT:SPA_LAT + 1, :]

    for h in range(SPA_HEADS):
        cs = slice(h * tq, (h + 1) * tq)
        o_lat = (acc_ref[0:SPA_LAT, cs] / l_fin[:, cs]).T
        oh = jnp.dot(o_lat.astype(BF16), wuv_ref[h], preferred_element_type=F32)
        g = g_ref[:, h * SPA_DV:(h + 1) * SPA_DV].astype(F32)
        o_ref[:, h * SPA_DV:(h + 1) * SPA_DV] = (oh * _silu(g)).astype(o_ref.dtype)


def _spa(hb, ht, wt, ikn, cn, cnt, wuv, tq, tk, topk):
    s = hb.shape[0]
    nq, nt = s // tq, s // tk
    const3 = lambda i: (0, 0, 0)
    return pl.pallas_call(
        functools.partial(_spa_kernel, tq=tq, tk=tk, topk=topk, idx_bits=(s - 1).bit_length()),
        out_shape=jax.ShapeDtypeStruct((s, SPA_HEADS * SPA_DV), BF16),
        grid=(nq,),
        in_specs=[
            pl.BlockSpec((IDX_HEADS * IDX_DIM, tq), lambda i: (HT_CIQ // (IDX_HEADS * IDX_DIM), i)),
            pl.BlockSpec((SPA_HEADS * SPA_LAT, tq), lambda i: (HT_CQ // (SPA_HEADS * SPA_LAT), i)),
            pl.BlockSpec((None, IDX_HEADS, tq), lambda i: (i, 0, 0)),
            pl.BlockSpec((tq, 1024), lambda i: (i, _HB_OFF["cg"] // 1024)),
            pl.BlockSpec((nt, tk, IDX_DIM), const3, pipeline_mode=pl.Buffered(1)),
            pl.BlockSpec((nt, tk, SPA_LAT), const3, pipeline_mode=pl.Buffered(1)),
            pl.BlockSpec((nt, SPA_LATP, tk), const3, pipeline_mode=pl.Buffered(1)),
            pl.BlockSpec((SPA_HEADS, SPA_LAT, SPA_DV), const3, pipeline_mode=pl.Buffered(1)),
        ],
        out_specs=pl.BlockSpec((tq, 1024), lambda i: (i, 0)),
        scratch_shapes=[pltpu.VMEM((nt + 1, tk, tq), I32), pltpu.VMEM((SPA_LATP, SPA_HEADS * tq), F32),
                        pltpu.VMEM((IDX_DIM, IDX_HEADS * tq), BF16), pltpu.VMEM((SPA_LAT, SPA_HEADS * tq), BF16),
                        pltpu.VMEM((tk, SPA_HEADS * tq), F32), pltpu.VMEM((tk, SPA_HEADS * tq), F32)],
        compiler_params=_cparams(("arbitrary",)),
        name="sparse_attn",
    )(ht, ht, wt, hb, ikn, cn, cnt, wuv)


def _out_kernel(oa_ref, ob_ref, oc_ref, x_ref, w_ref, lg_ref, lb_ref, y_ref, yb_ref, *, alpha):
    d = y_ref.shape[1]
    ncol = d // OUT_NC
    na, nb = oa_ref.shape[1], ob_ref.shape[1]
    oa, ob, oc = oa_ref[...], ob_ref[...], oc_ref[...]
    ssum = 0.0
    for c in range(ncol):
        cs = slice(c * OUT_NC, (c + 1) * OUT_NC)
        r = (alpha * x_ref[:, cs]
             + jnp.dot(oa, w_ref[0:na, cs], preferred_element_type=F32)
             + jnp.dot(ob, w_ref[na:na + nb, cs], preferred_element_type=F32)
             + jnp.dot(oc, w_ref[na + nb:, cs], preferred_element_type=F32))
        y_ref[:, cs] = r
        ssum = ssum + jnp.sum(r, axis=-1, keepdims=True)
    mu = ssum / d
    vsum = 0.0
    for c in range(ncol):
        cen = y_ref[:, c * OUT_NC:(c + 1) * OUT_NC] - mu
        vsum = vsum + jnp.sum(cen * cen, axis=-1, keepdims=True)
    rstd = lax.rsqrt(vsum / d + LN_EPS)
    for c in range(ncol):
        cs = slice(c * OUT_NC, (c + 1) * OUT_NC)
        y = (y_ref[:, cs] - mu) * rstd * lg_ref[:, cs] + lb_ref[:, cs]
        y_ref[:, cs] = y
        yb_ref[:, cs] = y.astype(BF16)


def _out_proj(oa, ob, oc, x, w, lg, lb, alpha, tm):
    s, d = x.shape
    row = lambda i: (i, 0)
    const = lambda i: (0, 0)
    return pl.pallas_call(
        functools.partial(_out_kernel, alpha=alpha),
        out_shape=(jax.ShapeDtypeStruct((s, d), F32), jax.ShapeDtypeStruct((s, d), BF16)),
        grid=(s // tm,),
        in_specs=[
            pl.BlockSpec((tm, oa.shape[1]), row),
            pl.BlockSpec((tm, ob.shape[1]), row),
            pl.BlockSpec((tm, oc.shape[1]), row),
            pl.BlockSpec((tm, d), row),
            pl.BlockSpec(w.shape, const, pipeline_mode=pl.Buffered(1)),
            pl.BlockSpec((1, d), const),
            pl.BlockSpec((1, d), const),
        ],
        out_specs=(pl.BlockSpec((tm, d), row), pl.BlockSpec((tm, d), row)),
        compiler_params=_cparams(("parallel",)),
        name="out_proj_ln",
    )(oa, ob, oc, x, w, lg, lb)


def kernel(x, w_in, w_out, gla_w_gate2, gla_b_gate, gla_norm_g, diff_lambda, diff_norm_g,
           spa_kv_norm_g, spa_ik_ln_g, spa_ik_ln_b, spa_w_uv, post_ln_g, post_ln_b):
    bsz, s, d = x.shape
    assert bsz == 1 and d == 4096 and w_in.shape[2] == sum(_IN_WIDTHS)
    depth = w_in.shape[0]
    alpha = (2.0 * depth) ** 0.25
    topk = min(IDX_TOPK_MAX, s // 4)
    tm_in = min(1024, s)
    t_diff = min(512, s)
    tq_s, tk_s = 128, min(1024, s)
    nq_s = s // tq_s

    xf = x[0]
    xb = xf.astype(BF16)
    for l in range(depth):
        wb, wf, wqt = _prep_w_in(w_in[l])
        tn = IN_TN
        hb = _matmul(xb, wb, BF16, tm_in, tn)
        hf = _matmul(xb, wf, F32, tm_in, tn)
        ht = _matmul_t(wqt, xb, BF16, tn, tm_in,
                       fold=((HT_CQ // tn, (HT_CQ + _IN_OFF["cq"][1]) // tn, _Q_FOLD["cq"]),
                             (HT_BQ // tn, (HT_BQ + _IN_OFF["bq"][1]) // tn, _Q_FOLD["bq"])))

        o_a = _gla(hb, hf, gla_w_gate2[l], gla_b_gate[l][None], gla_norm_g[l][None], min(256, s))

        lam_init = 0.8 - 0.6 * math.exp(-0.3 * l)
        o_b = _diff(hb, ht, diff_lambda[l], diff_norm_g[l][None], lam_init, t_diff)

        cn, cnt, ikn = _spa_norm(hf, spa_kv_norm_g[l][None], spa_ik_ln_g[l][None], spa_ik_ln_b[l][None], tk_s)
        ciw = lax.slice_in_dim(hf, HF_SMALL + SM_CIW, HF_SMALL + SM_CIW + IDX_HEADS, axis=1)
        wt = (ciw * (IDX_HEADS ** -0.5 * IDX_DIM ** -0.5)).reshape(nq_s, tq_s, IDX_HEADS).transpose(0, 2, 1)
        o_c = _spa(hb, ht, wt, ikn, cn, cnt, spa_w_uv[l].astype(BF16), tq_s, tk_s, topk)

        xf, xb = _out_proj(o_a, o_b, o_c, xf, w_out[l].astype(BF16), post_ln_g[l][None], post_ln_b[l][None],
                           alpha, min(128, s))
    return xf[None]
```

```python
import functools
import math

import jax
import jax.numpy as jnp
from jax import lax
from jax.experimental import pallas as pl
from jax.experimental.pallas import tpu as pltpu

F32 = jnp.float32
BF16 = jnp.bfloat16
I32 = jnp.int32

CHUNK = 64
GLA_HEADS, GLA_DK, GLA_DV, GLA_RANK, GLA_TAU = 4, 256, 512, 16, 16.0
DIFF_HEADS, DIFF_DQK, DIFF_DV = 4, 128, 256
SPA_HEADS, SPA_DV, SPA_LAT = 8, 128, 256
IDX_HEADS, IDX_DIM, IDX_TOPK_MAX = 16, 64, 256
SPA_LATP = SPA_LAT + 16
LN_EPS, RMS_EPS = 1e-5, 1e-6

_IN_WIDTHS = (1024, 1024, 2048, 16, 2048, 1024, 1024, 1024, 1024, 2048, 256, 1024, 64, 16, 1024)
_IN_NAMES = ("aq", "ak", "av", "aa", "ag", "bq", "bk", "bv", "bg", "cq", "ckv", "ciq", "cik", "ciw", "cg")
_IN_OFF = {}
_acc = 0
for _n, _w in zip(_IN_NAMES, _IN_WIDTHS):
    _IN_OFF[_n] = (_acc, _w)
    _acc += _w

_HB_ORDER = ("av", "ag", "aq", "bk", "bg", "cg")
_HB_OFF = {}
_acc = 0
for _n in _HB_ORDER:
    _HB_OFF[_n] = _acc
    _acc += _IN_OFF[_n][1]
HB_WIDTH = _acc
_HT_ORDER = ("cq", "ciq", "bq", "bv")
HT_CQ, HT_CIQ, HT_BQ, HT_BV, HT_WIDTH = 0, 2048, 3072, 4096, 5120
HF_AK, HF_CKV, HF_SMALL, HF_WIDTH = 0, 1024, 1280, 1536
SM_CIK, SM_AA, SM_CIW = 0, 64, 80

_Q_FOLD = {"bq": DIFF_DQK ** -0.5 * math.log2(math.e), "cq": SPA_LAT ** -0.5 * math.log2(math.e)}
IN_TN = 512
COUNT_ROWS = 512
BISECT_BLIND_BITS = 24
OUT_NC = 1024
VMEM_LIMIT = 56 * 1024 * 1024
NEG_BIG = -1e30
INT_MIN = -(2 ** 31)


def _cparams(sem):
    return pltpu.CompilerParams(dimension_semantics=sem, vmem_limit_bytes=VMEM_LIMIT)


def _silu(g):
    return g / (1.0 + jnp.exp(-g))


def _tile_scale(folds):
    j = pl.program_id(1)
    scale = jnp.float32(1.0)
    for lo, hi, c in folds:
        scale = jnp.where(jnp.logical_and(j >= lo, j < hi), c, scale)
    return scale


def _matmul_kernel(x_ref, w_ref, o_ref, *, fold):
    acc = jnp.dot(x_ref[...], w_ref[...], preferred_element_type=F32)
    o_ref[...] = (acc * _tile_scale(fold)).astype(o_ref.dtype)


def _matmul(x, w, out_dtype, tm, tn, fold=()):
    m, k = x.shape
    n = w.shape[1]
    return pl.pallas_call(
        functools.partial(_matmul_kernel, fold=fold),
        out_shape=jax.ShapeDtypeStruct((m, n), out_dtype),
        grid=(m // tm, n // tn),
        in_specs=[pl.BlockSpec((tm, k), lambda i, j: (i, 0)),
                  pl.BlockSpec((k, tn), lambda i, j: (0, j))],
        out_specs=pl.BlockSpec((tm, tn), lambda i, j: (i, j)),
        compiler_params=_cparams(("parallel", "parallel")),
        name="in_proj",
    )(x, w)


def _matmul_t_kernel(wt_ref, x_ref, o_ref, *, fold):
    acc = lax.dot_general(wt_ref[...], x_ref[...], (((1,), (1,)), ((), ())), preferred_element_type=F32)
    o_ref[...] = (acc * _tile_scale(fold)).astype(o_ref.dtype)


def _matmul_t(wt, x, out_dtype, tn, tm, fold=()):
    n, k = wt.shape
    m = x.shape[0]
    return pl.pallas_call(
        functools.partial(_matmul_t_kernel, fold=fold),
        out_shape=jax.ShapeDtypeStruct((n, m), out_dtype),
        grid=(m // tm, n // tn),
        in_specs=[pl.BlockSpec((tn, k), lambda i, j: (j, 0)),
                  pl.BlockSpec((tm, k), lambda i, j: (i, 0))],
        out_specs=pl.BlockSpec((tn, tm), lambda i, j: (j, i)),
        compiler_params=_cparams(("parallel", "parallel")),
        name="in_proj_t",
    )(wt, x)


def _prep_w_in(w):
    def cols(name):
        o, n = _IN_OFF[name]
        return w[:, o:o + n]
    d = w.shape[0]
    wb = jnp.concatenate([cols(n) for n in _HB_ORDER], axis=1).astype(BF16)
    small = jnp.concatenate([cols("cik"), cols("aa"), cols("ciw"), jnp.zeros((d, 32), w.dtype)], axis=1)
    wf = jnp.concatenate([cols("ak"), cols("ckv"), small, jnp.zeros((d, HF_WIDTH - HF_SMALL - 128), w.dtype)],
                         axis=1).astype(BF16)
    wt = jnp.concatenate([cols(n) for n in _HT_ORDER], axis=1).astype(BF16).T
    return wb, wf, wt


def _gla_kernel(v_ref, q_ref, g_ref, k_ref, sm_ref, w2_ref, b_ref, ng_ref, o_ref, st_ref, *, nchunk):
    @pl.when(pl.program_id(0) == 0)
    def _():
        st_ref[...] = jnp.zeros_like(st_ref)

    row = lax.broadcasted_iota(I32, (CHUNK, CHUNK), 0)
    col = lax.broadcasted_iota(I32, (CHUNK, CHUNK), 1)
    tri = (col <= row).astype(F32)
    chunks = [slice(c * CHUNK, (c + 1) * CHUNK) for c in range(nchunk)]
    kss = [slice(h * GLA_DK, (h + 1) * GLA_DK) for h in range(GLA_HEADS)]
    vss = [slice(h * GLA_DV, (h + 1) * GLA_DV) for h in range(GLA_HEADS)]

    aa = sm_ref[:, SM_AA:SM_AA + GLA_RANK].astype(BF16)
    z = jnp.dot(aa, w2_ref[...].astype(BF16), preferred_element_type=F32) + b_ref[...]
    log_a = (jnp.minimum(z, 0.0) - jnp.log1p(jnp.exp(-jnp.abs(z)))) * (1.0 / GLA_TAU)
    cums = [jnp.dot(tri, log_a[rs, :], preferred_element_type=F32, precision=lax.Precision.HIGHEST)
            for rs in chunks]
    tots = [cum[CHUNK - 1:CHUNK, :] for cum in cums]
    k_decs = [(k_ref[rs, :] * jnp.exp(tot - cum)).astype(BF16) for rs, cum, tot in zip(chunks, cums, tots)]
    decays = [jnp.exp(tot) for tot in tots]
    upds = [[lax.dot_general(v_ref[rs, vss[h]], k_dec[:, kss[h]], (((0,), (0,)), ((), ())),
                             preferred_element_type=F32) for h in range(GLA_HEADS)]
            for rs, k_dec in zip(chunks, k_decs)]
    states = [[None] * GLA_HEADS for _ in range(nchunk)]
    for h in range(GLA_HEADS):
        st = st_ref[h]
        for c in range(nchunk):
            st = st * decays[c][:, kss[h]] + upds[c][h]
            states[c][h] = st.astype(BF16)
        st_ref[h] = st
    outs = [[lax.dot_general(q_ref[rs, kss[h]], states[c][h], (((1,), (1,)), ((), ())),
                             preferred_element_type=F32) for h in range(GLA_HEADS)]
            for c, rs in enumerate(chunks)]
    for c, rs in enumerate(chunks):
        for h in range(GLA_HEADS):
            o = outs[c][h] * (GLA_DK ** -0.5)
            var = jnp.mean(o * o, axis=-1, keepdims=True)
            y = o * lax.rsqrt(var + RMS_EPS) * ng_ref[...]
            g = g_ref[rs, vss[h]].astype(F32)
            o_ref[rs, vss[h]] = (y * _silu(g)).astype(o_ref.dtype)


def _gla(hb, hf, w2, b, ng, tr):
    s = hb.shape[0]
    nchunk = tr // CHUNK
    return pl.pallas_call(
        functools.partial(_gla_kernel, nchunk=nchunk),
        out_shape=jax.ShapeDtypeStruct((s, GLA_HEADS * GLA_DV), BF16),
        grid=(s // tr,),
        in_specs=[
            pl.BlockSpec((tr, 2048), lambda i: (i, _HB_OFF["av"] // 2048)),
            pl.BlockSpec((tr, 1024), lambda i: (i, _HB_OFF["aq"] // 1024)),
            pl.BlockSpec((tr, 2048), lambda i: (i, _HB_OFF["ag"] // 2048)),
            pl.BlockSpec((tr, 1024), lambda i: (i, HF_AK // 1024)),
            pl.BlockSpec((tr, 128), lambda i: (i, HF_SMALL // 128)),
            pl.BlockSpec((GLA_RANK, 1024), lambda i: (0, 0)),
            pl.BlockSpec((1, 1024), lambda i: (0, 0)),
            pl.BlockSpec((1, GLA_DV), lambda i: (0, 0)),
        ],
        out_specs=pl.BlockSpec((tr, 2048), lambda i: (i, 0)),
        scratch_shapes=[pltpu.VMEM((GLA_HEADS, GLA_DV, GLA_DK), F32)],
        compiler_params=_cparams(("arbitrary",)),
        name="gla",
    )(hb, hb, hb, hf, hf, w2, b, ng)


def _diff_kernel(lam_ref, q_ref, k_ref, v_ref, g_ref, ng_ref, o_ref, acc_ref, sa_ref, sb_ref, *, t, lam_init):
    i = pl.program_id(1)
    lp = lam_ref[...]
    lam = (jnp.exp(jnp.sum(lp[0:1] * lp[1:2], axis=(0, 1), keepdims=True))
           - jnp.exp(jnp.sum(lp[2:3] * lp[3:4], axis=(0, 1), keepdims=True)) + lam_init)
    acc_ref[...] = jnp.zeros_like(acc_ref)
    maps = [slice(mp * DIFF_DQK, (mp + 1) * DIFF_DQK) for mp in range(2)]

    def qk(j, s_ref):
        kt = k_ref[pl.ds(pl.multiple_of(j * t, t), t), :]
        for mp in range(2):
            s_ref[mp] = jnp.dot(kt[:, maps[mp]], q_ref[maps[mp], :], preferred_element_type=F32)

    def soft_pv(j, s_ref, carry, masked):
        vt = v_ref[:, pl.ds(pl.multiple_of(j * t, t), t)]
        out = []
        for mp in range(2):
            m_old, l_old = carry[mp]
            s = s_ref[mp]
            if masked:
                kch = lax.broadcasted_iota(I32, (t, t), 0) // CHUNK
                qch = lax.broadcasted_iota(I32, (t, t), 1) // CHUNK
                s = jnp.where(kch <= qch, s, NEG_BIG)
            m_new = jnp.maximum(m_old, jnp.max(s, axis=0, keepdims=True))
            p = jnp.exp2(s - m_new)
            alpha = jnp.exp2(m_old - m_new)
            l_new = alpha * l_old + jnp.sum(p, axis=0, keepdims=True)
            acc_ref[mp] = acc_ref[mp] * alpha + jnp.dot(vt, p.astype(BF16), preferred_element_type=F32)
            out.append((m_new, l_new))
        return tuple(out)

    def pair(jj, carry):
        j0 = 2 * jj
        qk(j0 + 1, sb_ref)
        carry = soft_pv(j0, sa_ref, carry, False)
        qk(j0 + 2, sa_ref)
        return soft_pv(j0 + 1, sb_ref, carry, False)

    def even_tail(carry):
        return soft_pv(i, sa_ref, carry, True)

    def odd_tail(carry):
        qk(i, sb_ref)
        carry = soft_pv(i - 1, sa_ref, carry, False)
        return soft_pv(i, sb_ref, carry, True)

    qk(0, sa_ref)
    init = tuple((jnp.full((1, t), NEG_BIG, F32), jnp.zeros((1, t), F32)) for _ in range(2))
    (_, l0), (_, l1) = lax.cond(i % 2 == 0, even_tail, odd_tail, lax.fori_loop(0, i // 2, pair, init))
    o = (acc_ref[0] / l0 - lam * (acc_ref[1] / l1)).T
    var = jnp.mean(o * o, axis=-1, keepdims=True)
    y = o * lax.rsqrt(var + RMS_EPS) * ng_ref[...] * (1.0 - lam_init)
    g = g_ref[...].astype(F32)
    o_ref[...] = (y * _silu(g)).astype(o_ref.dtype)


def _diff(hb, ht, lam_p, ng, lam_init, t):
    s = hb.shape[0]
    w = DIFF_DV
    return pl.pallas_call(
        functools.partial(_diff_kernel, t=t, lam_init=lam_init),
        out_shape=jax.ShapeDtypeStruct((s, DIFF_HEADS * DIFF_DV), BF16),
        grid=(DIFF_HEADS, s // t),
        in_specs=[
            pl.BlockSpec((4, DIFF_DQK), lambda h, i: (0, 0)),
            pl.BlockSpec((w, t), lambda h, i: (HT_BQ // w + h, i)),
            pl.BlockSpec((s, w), lambda h, i: (0, _HB_OFF["bk"] // w + h)),
            pl.BlockSpec((w, s), lambda h, i: (HT_BV // w + h, 0)),
            pl.BlockSpec((t, w), lambda h, i: (i, _HB_OFF["bg"] // w + h)),
            pl.BlockSpec((1, DIFF_DV), lambda h, i: (0, 0)),
        ],
        out_specs=pl.BlockSpec((t, w), lambda h, i: (i, h)),
        scratch_shapes=[pltpu.VMEM((2, DIFF_DV, t), F32), pltpu.VMEM((2, t, t), F32), pltpu.VMEM((2, t, t), F32)],
        compiler_params=_cparams(("parallel", "arbitrary")),
        name="diff_attn",
    )(lam_p, ht, hb, ht, hb, ng)


def _spa_norm_kernel(ckv_ref, sm_ref, kvg_ref, lg_ref, lb_ref, cn_ref, cnt_ref, ikn_ref):
    c = ckv_ref[...]
    cn = c * lax.rsqrt(jnp.mean(c * c, axis=-1, keepdims=True) + RMS_EPS) * kvg_ref[...]
    cn_ref[...] = cn.astype(cn_ref.dtype)
    cnt_ref[0:SPA_LAT, :] = cn.T.astype(cnt_ref.dtype)
    pad_rows = cnt_ref.shape[0] - SPA_LAT
    ones_row = lax.broadcasted_iota(I32, (pad_rows, cnt_ref.shape[1]), 0) == 0
    cnt_ref[SPA_LAT:, :] = jnp.where(ones_row, 1.0, 0.0).astype(cnt_ref.dtype)
    ik = sm_ref[:, SM_CIK:SM_CIK + IDX_DIM]
    mu = jnp.mean(ik, axis=-1, keepdims=True)
    d = ik - mu
    var = jnp.mean(d * d, axis=-1, keepdims=True)
    ikn_ref[...] = (d * lax.rsqrt(var + LN_EPS) * lg_ref[...] + lb_ref[...]).astype(ikn_ref.dtype)


def _spa_norm(hf, kvg, lg, lb, tk):
    s = hf.shape[0]
    nt = s // tk
    return pl.pallas_call(
        _spa_norm_kernel,
        out_shape=(jax.ShapeDtypeStruct((nt, tk, SPA_LAT), BF16),
                   jax.ShapeDtypeStruct((nt, SPA_LATP, tk), BF16),
                   jax.ShapeDtypeStruct((nt, tk, IDX_DIM), BF16)),
        grid=(nt,),
        in_specs=[
            pl.BlockSpec((tk, SPA_LAT), lambda i: (i, HF_CKV // SPA_LAT)),
            pl.BlockSpec((tk, 128), lambda i: (i, HF_SMALL // 128)),
            pl.BlockSpec((1, SPA_LAT), lambda i: (0, 0)),
            pl.BlockSpec((1, IDX_DIM), lambda i: (0, 0)),
            pl.BlockSpec((1, IDX_DIM), lambda i: (0, 0)),
        ],
        out_specs=(pl.BlockSpec((None, tk, SPA_LAT), lambda i: (i, 0, 0)),
                   pl.BlockSpec((None, SPA_LATP, tk), lambda i: (i, 0, 0)),
                   pl.BlockSpec((None, tk, IDX_DIM), lambda i: (i, 0, 0))),
        compiler_params=_cparams(("parallel",)),
        name="spa_norm",
    )(hf, hf, kvg, lg, lb)


def _spa_kernel(iq_ref, q8_ref, w_ref, g_ref, ikn_ref, cn_ref, cnt_ref, wuv_ref, o_ref, keys_ref, acc_ref,
                iqt_ref, q8t_ref, sa_ref, sb_ref, *, tq, tk, topk, idx_bits):
    i = pl.program_id(0)
    for h in range(IDX_HEADS):
        iqt_ref[:, h * tq:(h + 1) * tq] = iq_ref[h * IDX_DIM:(h + 1) * IDX_DIM, :]
    for h in range(SPA_HEADS):
        q8t_ref[:, h * tq:(h + 1) * tq] = q8_ref[h * SPA_LAT:(h + 1) * SPA_LAT, :]
    n_t = (i * tq + tq + tk - 1) // tk
    qpos = i * tq + lax.broadcasted_iota(I32, (1, tq), 1)
    qch = qpos // CHUNK
    k_row = jnp.minimum(topk, (qch + 1) * CHUNK)
    hk = tk // 2

    def score_tile(j, masked):
        for half in range(2):
            ks = slice(half * hk, (half + 1) * hk)
            ikt = ikn_ref[j, ks, :]
            sc = jnp.zeros((hk, tq), F32)
            for hp in range(IDX_HEADS // 2):
                r = jnp.dot(ikt, iqt_ref[:, hp * 2 * tq:(hp + 1) * 2 * tq], preferred_element_type=F32)
                for hh in range(2):
                    h = hp * 2 + hh
                    sc = sc + jnp.maximum(r[:, hh * tq:(hh + 1) * tq], 0.0) * w_ref[h:h + 1, :]
            bits = pltpu.bitcast(sc, I32)
            key = bits ^ ((bits >> 31) & 0x7FFFFFFF)
            if masked:
                kch = (j * tk + half * hk + lax.broadcasted_iota(I32, (hk, tq), 0)) // CHUNK
                key = jnp.where(kch <= qch, key, INT_MIN)
            keys_ref[j, ks, :] = key

    def body_a(j, _):
        score_tile(j, False)
        return 0

    lax.fori_loop(0, n_t - 1, body_a, 0)
    score_tile(n_t - 1, True)

    keys_ref[n_t] = jnp.full((tk, tq), INT_MIN, I32)
    n_t2 = (n_t + 1) // 2

    cb = min(COUNT_ROWS, tk)

    def count_where(pred):
        def body(jj, cnt):
            for j in (2 * jj, 2 * jj + 1):
                for r0 in range(0, tk, cb):
                    ind = jnp.where(pred(keys_ref[j, r0:r0 + cb, :], j * tk + r0), 1, 0).astype(I32)
                    cnt = cnt + jnp.sum(ind.reshape(cb // 8, 8, tq), axis=0)
            return cnt
        cnt = lax.fori_loop(0, n_t2, body, jnp.zeros((8, tq), I32))
        return jnp.sum(cnt, axis=0, keepdims=True)

    def bis_cond(st):
        b, _, cnt_a = st
        return jnp.logical_and(b < 32, jnp.max(cnt_a - k_row) > 0)

    def bis_body(st):
        b, a, cnt_a = st
        cand_a = a | lax.shift_left(jnp.int32(1), 31 - b)
        cand = cand_a ^ INT_MIN
        cnt = count_where(lambda kt, first: kt >= cand)
        take = cnt >= k_row
        return b + 1, jnp.where(take, cand_a, a), jnp.where(take, cnt, cnt_a)

    st0 = (jnp.int32(0), jnp.zeros((1, tq), I32), jnp.zeros((1, tq), I32) + n_t * tk)
    st1 = lax.fori_loop(0, BISECT_BLIND_BITS, lambda _, st: bis_body(st), st0)
    _, a_fin, cnt_fin = lax.while_loop(bis_cond, bis_body, st1)
    thr = a_fin ^ INT_MIN

    @pl.when(jnp.max(cnt_fin - k_row) > 0)
    def _():
        need = k_row - count_where(lambda kt, first: kt > thr)

        def idx_of(first, rows):
            return first + lax.broadcasted_iota(I32, (rows, tq), 0)

        def idx_body(b, y):
            cand = y | lax.shift_left(jnp.int32(1), idx_bits - 1 - b)
            cnt = count_where(lambda kt, first: jnp.logical_and(kt == thr, idx_of(first, cb) < cand))
            return jnp.where(cnt < need, cand, y)

        y_keep = lax.fori_loop(0, idx_bits, idx_body, jnp.zeros((1, tq), I32))

        def fix_body(j, _):
            kt = keys_ref[j]
            drop = jnp.logical_and(kt == thr, idx_of(j * tk, tk) > y_keep)
            keys_ref[j] = jnp.where(drop, kt - 1, kt)
            return 0

        lax.fori_loop(0, n_t, fix_body, 0)

    acc_ref[...] = jnp.zeros_like(acc_ref)

    last = keys_ref.shape[0] - 2

    def qk(j, s_ref):
        s_ref[...] = jnp.dot(cn_ref[jnp.minimum(j, last)], q8t_ref[...], preferred_element_type=F32)

    def soft_pv(j, s_ref, m_old):
        madd = jnp.where(keys_ref[j] >= thr, 0.0, NEG_BIG)
        s = s_ref[...] + jnp.concatenate([madd] * SPA_HEADS, axis=1)
        m_new = jnp.maximum(m_old, jnp.max(s, axis=0, keepdims=True))
        p = jnp.exp2(s - m_new)
        alpha = jnp.exp2(m_old - m_new)
        acc_ref[...] = acc_ref[...] * alpha + jnp.dot(cnt_ref[jnp.minimum(j, last)], p.astype(BF16),
                                                      preferred_element_type=F32)
        return m_new

    def body_c(jj, m_run):
        j0 = 2 * jj
        qk(j0 + 1, sb_ref)
        m_run = soft_pv(j0, sa_ref, m_run)
        qk(j0 + 2, sa_ref)
        return soft_pv(j0 + 1, sb_ref, m_run)

    qk(0, sa_ref)
    m_run = lax.fori_loop(0, n_t // 2, body_c, jnp.full((1, SPA_HEADS * tq), NEG_BIG, F32))

    @pl.when(n_t % 2 == 1)
    def _():
        soft_pv(n_t - 1, sa_ref, m_run)

    l_fin = acc_ref[SPA_LAT:SPA_LAT + 1, :]

    for h in range(SPA_HEADS):
        cs = slice(h * tq, (h + 1) * tq)
        o_lat = (acc_ref[0:SPA_LAT, cs] / l_fin[:, cs]).T
        oh = jnp.dot(o_lat.astype(BF16), wuv_ref[h], preferred_element_type=F32)
        g = g_ref[:, h * SPA_DV:(h + 1) * SPA_DV].astype(F32)
        o_ref[:, h * SPA_DV:(h + 1) * SPA_DV] = (oh * _silu(g)).astype(o_ref.dtype)


def _spa(hb, ht, wt, ikn, cn, cnt, wuv, tq, tk, topk):
    s = hb.shape[0]
    nq, nt = s // tq, s // tk
    const3 = lambda i: (0, 0, 0)
    return pl.pallas_call(
        functools.partial(_spa_kernel, tq=tq, tk=tk, topk=topk, idx_bits=(s - 1).bit_length()),
        out_shape=jax.ShapeDtypeStruct((s, SPA_HEADS * SPA_DV), BF16),
        grid=(nq,),
        in_specs=[
            pl.BlockSpec((IDX_HEADS * IDX_DIM, tq), lambda i: (HT_CIQ // (IDX_HEADS * IDX_DIM), i)),
            pl.BlockSpec((SPA_HEADS * SPA_LAT, tq), lambda i: (HT_CQ // (SPA_HEADS * SPA_LAT), i)),
            pl.BlockSpec((None, IDX_HEADS, tq), lambda i: (i, 0, 0)),
            pl.BlockSpec((tq, 1024), lambda i: (i, _HB_OFF["cg"] // 1024)),
            pl.BlockSpec((nt, tk, IDX_DIM), const3, pipeline_mode=pl.Buffered(1)),
            pl.BlockSpec((nt, tk, SPA_LAT), const3, pipeline_mode=pl.Buffered(1)),
            pl.BlockSpec((nt, SPA_LATP, tk), const3, pipeline_mode=pl.Buffered(1)),
            pl.BlockSpec((SPA_HEADS, SPA_LAT, SPA_DV), const3, pipeline_mode=pl.Buffered(1)),
        ],
        out_specs=pl.BlockSpec((tq, 1024), lambda i: (i, 0)),
        scratch_shapes=[pltpu.VMEM((nt + 1, tk, tq), I32), pltpu.VMEM((SPA_LATP, SPA_HEADS * tq), F32),
                        pltpu.VMEM((IDX_DIM, IDX_HEADS * tq), BF16), pltpu.VMEM((SPA_LAT, SPA_HEADS * tq), BF16),
                        pltpu.VMEM((tk, SPA_HEADS * tq), F32), pltpu.VMEM((tk, SPA_HEADS * tq), F32)],
        compiler_params=_cparams(("arbitrary",)),
        name="sparse_attn",
    )(ht, ht, wt, hb, ikn, cn, cnt, wuv)


def _out_kernel(oa_ref, ob_ref, oc_ref, x_ref, w_ref, lg_ref, lb_ref, y_ref, yb_ref, *, alpha):
    d = y_ref.shape[1]
    ncol = d // OUT_NC
    na, nb = oa_ref.shape[1], ob_ref.shape[1]
    oa, ob, oc = oa_ref[...], ob_ref[...], oc_ref[...]
    ssum = 0.0
    for c in range(ncol):
        cs = slice(c * OUT_NC, (c + 1) * OUT_NC)
        r = (alpha * x_ref[:, cs]
             + jnp.dot(oa, w_ref[0:na, cs], preferred_element_type=F32)
             + jnp.dot(ob, w_ref[na:na + nb, cs], preferred_element_type=F32)
             + jnp.dot(oc, w_ref[na + nb:, cs], preferred_element_type=F32))
        y_ref[:, cs] = r
        ssum = ssum + jnp.sum(r, axis=-1, keepdims=True)
    mu = ssum / d
    vsum = 0.0
    for c in range(ncol):
        cen = y_ref[:, c * OUT_NC:(c + 1) * OUT_NC] - mu
        vsum = vsum + jnp.sum(cen * cen, axis=-1, keepdims=True)
    rstd = lax.rsqrt(vsum / d + LN_EPS)
    for c in range(ncol):
        cs = slice(c * OUT_NC, (c + 1) * OUT_NC)
        y = (y_ref[:, cs] - mu) * rstd * lg_ref[:, cs] + lb_ref[:, cs]
        y_ref[:, cs] = y
        yb_ref[:, cs] = y.astype(BF16)


def _out_proj(oa, ob, oc, x, w, lg, lb, alpha, tm):
    s, d = x.shape
    row = lambda i: (i, 0)
    const = lambda i: (0, 0)
    return pl.pallas_call(
        functools.partial(_out_kernel, alpha=alpha),
        out_shape=(jax.ShapeDtypeStruct((s, d), F32), jax.ShapeDtypeStruct((s, d), BF16)),
        grid=(s // tm,),
        in_specs=[
            pl.BlockSpec((tm, oa.shape[1]), row),
            pl.BlockSpec((tm, ob.shape[1]), row),
            pl.BlockSpec((tm, oc.shape[1]), row),
            pl.BlockSpec((tm, d), row),
            pl.BlockSpec(w.shape, const, pipeline_mode=pl.Buffered(1)),
            pl.BlockSpec((1, d), const),
            pl.BlockSpec((1, d), const),
        ],
        out_specs=(pl.BlockSpec((tm, d), row), pl.BlockSpec((tm, d), row)),
        compiler_params=_cparams(("parallel",)),
        name="out_proj_ln",
    )(oa, ob, oc, x, w, lg, lb)


def kernel(x, w_in, w_out, gla_w_gate2, gla_b_gate, gla_norm_g, diff_lambda, diff_norm_g,
           spa_kv_norm_g, spa_ik_ln_g, spa_ik_ln_b, spa_w_uv, post_ln_g, post_ln_b):
    bsz, s, d = x.shape
    assert bsz == 1 and d == 4096 and w_in.shape[2] == sum(_IN_WIDTHS)
    depth = w_in.shape[0]
    alpha = (2.0 * depth) ** 0.25
    topk = min(IDX_TOPK_MAX, s // 4)
    tm_in = min(1024, s)
    t_diff = min(512, s)
    tq_s, tk_s = 128, min(1024, s)
    nq_s = s // tq_s

    xf = x[0]
    xb = xf.astype(BF16)
    for l in range(depth):
        wb, wf, wqt = _prep_w_in(w_in[l])
        tn = IN_TN
        hb = _matmul(xb, wb, BF16, tm_in, tn)
        hf = _matmul(xb, wf, F32, tm_in, tn)
        ht = _matmul_t(wqt, xb, BF16, tn, tm_in,
                       fold=((HT_CQ // tn, (HT_CQ + _IN_OFF["cq"][1]) // tn, _Q_FOLD["cq"]),
                             (HT_BQ // tn, (HT_BQ + _IN_OFF["bq"][1]) // tn, _Q_FOLD["bq"])))

        o_a = _gla(hb, hf, gla_w_gate2[l], gla_b_gate[l][None], gla_norm_g[l][None], min(256, s))

        lam_init = 0.8 - 0.6 * math.exp(-0.3 * l)
        o_b = _diff(hb, ht, diff_lambda[l], diff_norm_g[l][None], lam_init, t_diff)

        cn, cnt, ikn = _spa_norm(hf, spa_kv_norm_g[l][None], spa_ik_ln_g[l][None], spa_ik_ln_b[l][None], tk_s)
        ciw = lax.slice_in_dim(hf, HF_SMALL + SM_CIW, HF_SMALL + SM_CIW + IDX_HEADS, axis=1)
        wt = (ciw * (IDX_HEADS ** -0.5 * IDX_DIM ** -0.5)).reshape(nq_s, tq_s, IDX_HEADS).transpose(0, 2, 1)
        o_c = _spa(hb, ht, wt, ikn, cn, cnt, spa_w_uv[l].astype(BF16), tq_s, tk_s, topk)

        xf, xb = _out_proj(o_a, o_b, o_c, xf, w_out[l].astype(BF16), post_ln_g[l][None], post_ln_b[l][None],
                           alpha, min(128, s))
    return xf[None]
```

```python
import functools
import math

import jax
import jax.numpy as jnp
from jax import lax
from jax.experimental import pallas as pl
from jax.experimental.pallas import tpu as pltpu

F32 = jnp.float32
BF16 = jnp.bfloat16
I32 = jnp.int32

CHUNK = 64
GLA_HEADS, GLA_DK, GLA_DV, GLA_RANK, GLA_TAU = 4, 256, 512, 16, 16.0
DIFF_HEADS, DIFF_DQK, DIFF_DV = 4, 128, 256
SPA_HEADS, SPA_DV, SPA_LAT = 8, 128, 256
IDX_HEADS, IDX_DIM, IDX_TOPK_MAX = 16, 64, 256
SPA_LATP = SPA_LAT + 16
LN_EPS, RMS_EPS = 1e-5, 1e-6

_IN_WIDTHS = (1024, 1024, 2048, 16, 2048, 1024, 1024, 1024, 1024, 2048, 256, 1024, 64, 16, 1024)
_IN_NAMES = ("aq", "ak", "av", "aa", "ag", "bq", "bk", "bv", "bg", "cq", "ckv", "ciq", "cik", "ciw", "cg")
_IN_OFF = {}
_acc = 0
for _n, _w in zip(_IN_NAMES, _IN_WIDTHS):
    _IN_OFF[_n] = (_acc, _w)
    _acc += _w

_HB_ORDER = ("av", "ag", "aq", "bk", "bg", "cg")
_HB_OFF = {}
_acc = 0
for _n in _HB_ORDER:
    _HB_OFF[_n] = _acc
    _acc += _IN_OFF[_n][1]
HB_WIDTH = _acc
_HT_ORDER = ("cq", "ciq", "bq", "bv")
HT_CQ, HT_CIQ, HT_BQ, HT_BV, HT_WIDTH = 0, 2048, 3072, 4096, 5120
HF_AK, HF_CKV, HF_SMALL, HF_WIDTH = 0, 1024, 1280, 1536
SM_CIK, SM_AA, SM_CIW = 0, 64, 80

_Q_FOLD = {"bq": DIFF_DQK ** -0.5 * math.log2(math.e), "cq": SPA_LAT ** -0.5 * math.log2(math.e)}
IN_TN = 512
COUNT_ROWS = 512
BISECT_BLIND_BITS = 24
OUT_NC = 1024
VMEM_LIMIT = 56 * 1024 * 1024
NEG_BIG = -1e30
INT_MIN = -(2 ** 31)


def _cparams(sem):
    return pltpu.CompilerParams(dimension_semantics=sem, vmem_limit_bytes=VMEM_LIMIT)


def _silu(g):
    return g / (1.0 + jnp.exp(-g))


def _tile_scale(folds):
    j = pl.program_id(1)
    scale = jnp.float32(1.0)
    for lo, hi, c in folds:
        scale = jnp.where(jnp.logical_and(j >= lo, j < hi), c, scale)
    return scale


def _matmul_kernel(x_ref, w_ref, o_ref, *, fold):
    acc = jnp.dot(x_ref[...], w_ref[...], preferred_element_type=F32)
    o_ref[...] = (acc * _tile_scale(fold)).astype(o_ref.dtype)


def _matmul(x, w, out_dtype, tm, tn, fold=()):
    m, k = x.shape
    n = w.shape[1]
    return pl.pallas_call(
        functools.partial(_matmul_kernel, fold=fold),
        out_shape=jax.ShapeDtypeStruct((m, n), out_dtype),
        grid=(m // tm, n // tn),
        in_specs=[pl.BlockSpec((tm, k), lambda i, j: (i, 0)),
                  pl.BlockSpec((k, tn), lambda i, j: (0, j))],
        out_specs=pl.BlockSpec((tm, tn), lambda i, j: (i, j)),
        compiler_params=_cparams(("parallel", "parallel")),
        name="in_proj",
    )(x, w)


def _matmul_t_kernel(wt_ref, x_ref, o_ref, *, fold):
    acc = lax.dot_general(wt_ref[...], x_ref[...], (((1,), (1,)), ((), ())), preferred_element_type=F32)
    o_ref[...] = (acc * _tile_scale(fold)).astype(o_ref.dtype)


def _matmul_t(wt, x, out_dtype, tn, tm, fold=()):
    n, k = wt.shape
    m = x.shape[0]
    return pl.pallas_call(
        functools.partial(_matmul_t_kernel, fold=fold),
        out_shape=jax.ShapeDtypeStruct((n, m), out_dtype),
        grid=(m // tm, n // tn),
        in_specs=[pl.BlockSpec((tn, k), lambda i, j: (j, 0)),
                  pl.BlockSpec((tm, k), lambda i, j: (i, 0))],
        out_specs=pl.BlockSpec((tn, tm), lambda i, j: (j, i)),
        compiler_params=_cparams(("parallel", "parallel")),
        name="in_proj_t",
    )(wt, x)


def _prep_w_in(w):
    def cols(name):
        o, n = _IN_OFF[name]
        return w[:, o:o + n]
    d = w.shape[0]
    wb = jnp.concatenate([cols(n) for n in _HB_ORDER], axis=1).astype(BF16)
    small = jnp.concatenate([cols("cik"), cols("aa"), cols("ciw"), jnp.zeros((d, 32), w.dtype)], axis=1)
    wf = jnp.concatenate([cols("ak"), cols("ckv"), small, jnp.zeros((d, HF_WIDTH - HF_SMALL - 128), w.dtype)],
                         axis=1).astype(BF16)
    wt = jnp.concatenate([cols(n) for n in _HT_ORDER], axis=1).astype(BF16).T
    return wb, wf, wt


def _gla_kernel(v_ref, q_ref, g_ref, k_ref, sm_ref, w2_ref, b_ref, ng_ref, o_ref, st_ref, *, nchunk):
    @pl.when(pl.program_id(0) == 0)
    def _():
        st_ref[...] = jnp.zeros_like(st_ref)

    row = lax.broadcasted_iota(I32, (CHUNK, CHUNK), 0)
    col = lax.broadcasted_iota(I32, (CHUNK, CHUNK), 1)
    tri = (col <= row).astype(F32)
    chunks = [slice(c * CHUNK, (c + 1) * CHUNK) for c in range(nchunk)]
    kss = [slice(h * GLA_DK, (h + 1) * GLA_DK) for h in range(GLA_HEADS)]
    vss = [slice(h * GLA_DV, (h + 1) * GLA_DV) for h in range(GLA_HEADS)]

    aa = sm_ref[:, SM_AA:SM_AA + GLA_RANK].astype(BF16)
    z = jnp.dot(aa, w2_ref[...].astype(BF16), preferred_element_type=F32) + b_ref[...]
    log_a = (jnp.minimum(z, 0.0) - jnp.log1p(jnp.exp(-jnp.abs(z)))) * (1.0 / GLA_TAU)
    cums = [jnp.dot(tri, log_a[rs, :], preferred_element_type=F32, precision=lax.Precision.HIGHEST)
            for rs in chunks]
    tots = [cum[CHUNK - 1:CHUNK, :] for cum in cums]
    k_decs = [(k_ref[rs, :] * jnp.exp(tot - cum)).astype(BF16) for rs, cum, tot in zip(chunks, cums, tots)]
    decays = [jnp.exp(tot) for tot in tots]
    upds = [[lax.dot_general(v_ref[rs, vss[h]], k_dec[:, kss[h]], (((0,), (0,)), ((), ())),
                             preferred_element_type=F32) for h in range(GLA_HEADS)]
            for rs, k_dec in zip(chunks, k_decs)]
    states = [[None] * GLA_HEADS for _ in range(nchunk)]
    for h in range(GLA_HEADS):
        st = st_ref[h]
        for c in range(nchunk):
            st = st * decays[c][:, kss[h]] + upds[c][h]
            states[c][h] = st.astype(BF16)
        st_ref[h] = st
    outs = [[lax.dot_general(q_ref[rs, kss[h]], states[c][h], (((1,), (1,)), ((), ())),
                             preferred_element_type=F32) for h in range(GLA_HEADS)]
            for c, rs in enumerate(chunks)]
    for c, rs in enumerate(chunks):
        for h in range(GLA_HEADS):
            o = outs[c][h] * (GLA_DK ** -0.5)
            var = jnp.mean(o * o, axis=-1, keepdims=True)
            y = o * lax.rsqrt(var + RMS_EPS) * ng_ref[...]
            g = g_ref[rs, vss[h]].astype(F32)
            o_ref[rs, vss[h]] = (y * _silu(g)).astype(o_ref.dtype)


def _gla(hb, hf, w2, b, ng, tr):
    s = hb.shape[0]
    nchunk = tr // CHUNK
    return pl.pallas_call(
        functools.partial(_gla_kernel, nchunk=nchunk),
        out_shape=jax.ShapeDtypeStruct((s, GLA_HEADS * GLA_DV), BF16),
        grid=(s // tr,),
        in_specs=[
            pl.BlockSpec((tr, 2048), lambda i: (i, _HB_OFF["av"] // 2048)),
            pl.BlockSpec((tr, 1024), lambda i: (i, _HB_OFF["aq"] // 1024)),
            pl.BlockSpec((tr, 2048), lambda i: (i, _HB_OFF["ag"] // 2048)),
            pl.BlockSpec((tr, 1024), lambda i: (i, HF_AK // 1024)),
            pl.BlockSpec((tr, 128), lambda i: (i, HF_SMALL // 128)),
            pl.BlockSpec((GLA_RANK, 1024), lambda i: (0, 0)),
            pl.BlockSpec((1, 1024), lambda i: (0, 0)),
            pl.BlockSpec((1, GLA_DV), lambda i: (0, 0)),
        ],
        out_specs=pl.BlockSpec((tr, 2048), lambda i: (i, 0)),
        scratch_shapes=[pltpu.VMEM((GLA_HEADS, GLA_DV, GLA_DK), F32)],
        compiler_params=_cparams(("arbitrary",)),
        name="gla",
    )(hb, hb, hb, hf, hf, w2, b, ng)


def _diff_kernel(lam_ref, q_ref, k_ref, v_ref, g_ref, ng_ref, o_ref, acc_ref, sa_ref, sb_ref, *, t, lam_init):
    i = pl.program_id(1)
    lp = lam_ref[...]
    lam = (jnp.exp(jnp.sum(lp[0:1] * lp[1:2], axis=(0, 1), keepdims=True))
           - jnp.exp(jnp.sum(lp[2:3] * lp[3:4], axis=(0, 1), keepdims=True)) + lam_init)
    acc_ref[...] = jnp.zeros_like(acc_ref)
    maps = [slice(mp * DIFF_DQK, (mp + 1) * DIFF_DQK) for mp in range(2)]

    def qk(j, s_ref):
        kt = k_ref[pl.ds(pl.multiple_of(j * t, t), t), :]
        for mp in range(2):
            s_ref[mp] = jnp.dot(kt[:, maps[mp]], q_ref[maps[mp], :], preferred_element_type=F32)

    def soft_pv(j, s_ref, carry, masked):
        vt = v_ref[:, pl.ds(pl.multiple_of(j * t, t), t)]
        out = []
        for mp in range(2):
            m_old, l_old = carry[mp]
            s = s_ref[mp]
            if masked:
                kch = lax.broadcasted_iota(I32, (t, t), 0) // CHUNK
                qch = lax.broadcasted_iota(I32, (t, t), 1) // CHUNK
                s = jnp.where(kch <= qch, s, NEG_BIG)
            m_new = jnp.maximum(m_old, jnp.max(s, axis=0, keepdims=True))
            p = jnp.exp2(s - m_new)
            alpha = jnp.exp2(m_old - m_new)
            l_new = alpha * l_old + jnp.sum(p, axis=0, keepdims=True)
            acc_ref[mp] = acc_ref[mp] * alpha + jnp.dot(vt, p.astype(BF16), preferred_element_type=F32)
            out.append((m_new, l_new))
        return tuple(out)

    def pair(jj, carry):
        j0 = 2 * jj
        qk(j0 + 1, sb_ref)
        carry = soft_pv(j0, sa_ref, carry, False)
        qk(j0 + 2, sa_ref)
        return soft_pv(j0 + 1, sb_ref, carry, False)

    def even_tail(carry):
        return soft_pv(i, sa_ref, carry, True)

    def odd_tail(carry):
        qk(i, sb_ref)
        carry = soft_pv(i - 1, sa_ref, carry, False)
        return soft_pv(i, sb_ref, carry, True)

    qk(0, sa_ref)
    init = tuple((jnp.full((1, t), NEG_BIG, F32), jnp.zeros((1, t), F32)) for _ in range(2))
    (_, l0), (_, l1) = lax.cond(i % 2 == 0, even_tail, odd_tail, lax.fori_loop(0, i // 2, pair, init))
    o = (acc_ref[0] / l0 - lam * (acc_ref[1] / l1)).T
    var = jnp.mean(o * o, axis=-1, keepdims=True)
    y = o * lax.rsqrt(var + RMS_EPS) * ng_ref[...] * (1.0 - lam_init)
    g = g_ref[...].astype(F32)
    o_ref[...] = (y * _silu(g)).astype(o_ref.dtype)


def _diff(hb, ht, lam_p, ng, lam_init, t):
    s = hb.shape[0]
    w = DIFF_DV
    return pl.pallas_call(
        functools.partial(_diff_kernel, t=t, lam_init=lam_init),
        out_shape=jax.ShapeDtypeStruct((s, DIFF_HEADS * DIFF_DV), BF16),
        grid=(DIFF_HEADS, s // t),
        in_specs=[
            pl.BlockSpec((4, DIFF_DQK), lambda h, i: (0, 0)),
            pl.BlockSpec((w, t), lambda h, i: (HT_BQ // w + h, i)),
            pl.BlockSpec((s, w), lambda h, i: (0, _HB_OFF["bk"] // w + h), pipeline_mode=pl.Buffered(1)),
            pl.BlockSpec((w, s), lambda h, i: (HT_BV // w + h, 0), pipeline_mode=pl.Buffered(1)),
            pl.BlockSpec((t, w), lambda h, i: (i, _HB_OFF["bg"] // w + h)),
            pl.BlockSpec((1, DIFF_DV), lambda h, i: (0, 0)),
        ],
        out_specs=pl.BlockSpec((t, w), lambda h, i: (i, h)),
        scratch_shapes=[pltpu.VMEM((2, DIFF_DV, t), F32), pltpu.VMEM((2, t, t), F32), pltpu.VMEM((2, t, t), F32)],
        compiler_params=_cparams(("parallel", "arbitrary")),
        name="diff_attn",
    )(lam_p, ht, hb, ht, hb, ng)


def _spa_norm_kernel(ckv_ref, sm_ref, kvg_ref, lg_ref, lb_ref, cn_ref, cnt_ref, ikn_ref):
    c = ckv_ref[...]
    cn = c * lax.rsqrt(jnp.mean(c * c, axis=-1, keepdims=True) + RMS_EPS) * kvg_ref[...]
    cn_ref[...] = cn.astype(cn_ref.dtype)
    cnt_ref[0:SPA_LAT, :] = cn.T.astype(cnt_ref.dtype)
    pad_rows = cnt_ref.shape[0] - SPA_LAT
    ones_row = lax.broadcasted_iota(I32, (pad_rows, cnt_ref.shape[1]), 0) == 0
    cnt_ref[SPA_LAT:, :] = jnp.where(ones_row, 1.0, 0.0).astype(cnt_ref.dtype)
    ik = sm_ref[:, SM_CIK:SM_CIK + IDX_DIM]
    mu = jnp.mean(ik, axis=-1, keepdims=True)
    d = ik - mu
    var = jnp.mean(d * d, axis=-1, keepdims=True)
    ikn_ref[...] = (d * lax.rsqrt(var + LN_EPS) * lg_ref[...] + lb_ref[...]).astype(ikn_ref.dtype)


def _spa_norm(hf, kvg, lg, lb, tk):
    s = hf.shape[0]
    nt = s // tk
    return pl.pallas_call(
        _spa_norm_kernel,
        out_shape=(jax.ShapeDtypeStruct((nt, tk, SPA_LAT), BF16),
                   jax.ShapeDtypeStruct((nt, SPA_LATP, tk), BF16),
                   jax.ShapeDtypeStruct((nt, tk, IDX_DIM), BF16)),
        grid=(nt,),
        in_specs=[
            pl.BlockSpec((tk, SPA_LAT), lambda i: (i, HF_CKV // SPA_LAT)),
            pl.BlockSpec((tk, 128), lambda i: (i, HF_SMALL // 128)),
            pl.BlockSpec((1, SPA_LAT), lambda i: (0, 0)),
            pl.BlockSpec((1, IDX_DIM), lambda i: (0, 0)),
            pl.BlockSpec((1, IDX_DIM), lambda i: (0, 0)),
        ],
        out_specs=(pl.BlockSpec((None, tk, SPA_LAT), lambda i: (i, 0, 0)),
                   pl.BlockSpec((None, SPA_LATP, tk), lambda i: (i, 0, 0)),
                   pl.BlockSpec((None, tk, IDX_DIM), lambda i: (i, 0, 0))),
        compiler_params=_cparams(("parallel",)),
        name="spa_norm",
    )(hf, hf, kvg, lg, lb)


def _spa_kernel(iq_ref, q8_ref, w_ref, g_ref, ikn_ref, cn_ref, cnt_ref, wuv_ref, o_ref, keys_ref, acc_ref,
                iqt_ref, q8t_ref, sa_ref, sb_ref, *, tq, tk, topk, idx_bits):
    i = pl.program_id(0)
    for h in range(IDX_HEADS):
        iqt_ref[:, h * tq:(h + 1) * tq] = iq_ref[h * IDX_DIM:(h + 1) * IDX_DIM, :]
    for h in range(SPA_HEADS):
        q8t_ref[:, h * tq:(h + 1) * tq] = q8_ref[h * SPA_LAT:(h + 1) * SPA_LAT, :]
    n_t = (i * tq + tq + tk - 1) // tk
    qpos = i * tq + lax.broadcasted_iota(I32, (1, tq), 1)
    qch = qpos // CHUNK
    k_row = jnp.minimum(topk, (qch + 1) * CHUNK)
    hk = tk // 2

    def score_tile(j, masked):
        for half in range(2):
            ks = slice(half * hk, (half + 1) * hk)
            ikt = ikn_ref[j, ks, :]
            sc = jnp.zeros((hk, tq), F32)
            for hp in range(IDX_HEADS // 2):
                r = jnp.dot(ikt, iqt_ref[:, hp * 2 * tq:(hp + 1) * 2 * tq], preferred_element_type=F32)
                for hh in range(2):
                    h = hp * 2 + hh
                    sc = sc + jnp.maximum(r[:, hh * tq:(hh + 1) * tq], 0.0) * w_ref[h:h + 1, :]
            bits = pltpu.bitcast(sc, I32)
            key = bits ^ ((bits >> 31) & 0x7FFFFFFF)
            if masked:
                kch = (j * tk + half * hk + lax.broadcasted_iota(I32, (hk, tq), 0)) // CHUNK
                key = jnp.where(kch <= qch, key, INT_MIN)
            keys_ref[j, ks, :] = key

    def body_a(j, _):
        score_tile(j, False)
        return 0

    lax.fori_loop(0, n_t - 1, body_a, 0)
    score_tile(n_t - 1, True)

    keys_ref[n_t] = jnp.full((tk, tq), INT_MIN, I32)
    n_t2 = (n_t + 1) // 2

    cb = min(COUNT_ROWS, tk)

    def count_where(pred):
        def body(jj, cnt):
            for j in (2 * jj, 2 * jj + 1):
                for r0 in range(0, tk, cb):
                    ind = jnp.where(pred(keys_ref[j, r0:r0 + cb, :], j * tk + r0), 1, 0).astype(I32)
                    cnt = cnt + jnp.sum(ind.reshape(cb // 8, 8, tq), axis=0)
            return cnt
        cnt = lax.fori_loop(0, n_t2, body, jnp.zeros((8, tq), I32))
        return jnp.sum(cnt, axis=0, keepdims=True)

    def bis_cond(st):
        b, _, cnt_a = st
        return jnp.logical_and(b < 32, jnp.max(cnt_a - k_row) > 0)

    def bis_body(st):
        b, a, cnt_a = st
        cand_a = a | lax.shift_left(jnp.int32(1), 31 - b)
        cand = cand_a ^ INT_MIN
        cnt = count_where(lambda kt, first: kt >= cand)
        take = cnt >= k_row
        return b + 1, jnp.where(take, cand_a, a), jnp.where(take, cnt, cnt_a)

    st0 = (jnp.int32(0), jnp.zeros((1, tq), I32), jnp.zeros((1, tq), I32) + n_t * tk)
    st1 = lax.fori_loop(0, BISECT_BLIND_BITS, lambda _, st: bis_body(st), st0)
    _, a_fin, cnt_fin = lax.while_loop(bis_cond, bis_body, st1)
    thr = a_fin ^ INT_MIN

    @pl.when(jnp.max(cnt_fin - k_row) > 0)
    def _():
        need = k_row - count_where(lambda kt, first: kt > thr)

        def idx_of(first, rows):
            return first + lax.broadcasted_iota(I32, (rows, tq), 0)

        def idx_body(b, y):
            cand = y | lax.shift_left(jnp.int32(1), idx_bits - 1 - b)
            cnt = count_where(lambda kt, first: jnp.logical_and(kt == thr, idx_of(first, cb) < cand))
            return jnp.where(cnt < need, cand, y)

        y_keep = lax.fori_loop(0, idx_bits, idx_body, jnp.zeros((1, tq), I32))

        def fix_body(j, _):
            kt = keys_ref[j]
            drop = jnp.logical_and(kt == thr, idx_of(j * tk, tk) > y_keep)
            keys_ref[j] = jnp.where(drop, kt - 1, kt)
            return 0

        lax.fori_loop(0, n_t, fix_body, 0)

    acc_ref[...] = jnp.zeros_like(acc_ref)

    last = keys_ref.shape[0] - 2

    def qk(j, s_ref):
        s_ref[...] = jnp.dot(cn_ref[jnp.minimum(j, last)], q8t_ref[...], preferred_element_type=F32)

    def soft_pv(j, s_ref, m_old):
        madd = jnp.where(keys_ref[j] >= thr, 0.0, NEG_BIG)
        s = s_ref[...] + jnp.concatenate([madd] * SPA_HEADS, axis=1)
        m_new = jnp.maximum(m_old, jnp.max(s, axis=0, keepdims=True))
        p = jnp.exp2(s - m_new)
        alpha = jnp.exp2(m_old - m_new)
        acc_ref[...] = acc_ref[...] * alpha + jnp.dot(cnt_ref[jnp.minimum(j, last)], p.astype(BF16),
                                                      preferred_element_type=F32)
        return m_new

    def body_c(jj, m_run):
        j0 = 2 * jj
        qk(j0 + 1, sb_ref)
        m_run = soft_pv(j0, sa_ref, m_run)
        qk(j0 + 2, sa_ref)
        return soft_pv(j0 + 1, sb_ref, m_run)

    qk(0, sa_ref)
    m_run = lax.fori_loop(0, n_t // 2, body_c, jnp.full((1, SPA_HEADS * tq), NEG_BIG, F32))

    @pl.when(n_t % 2 == 1)
    def _():
        soft_pv(n_t - 1, sa_ref, m_run)

    l_fin = acc_ref[SPA_LAT:SPA_LAT + 1, :]

    for h in range(SPA_HEADS):
        cs = slice(h * tq, (h + 1) * tq)
        o_lat = (acc_ref[0:SPA_LAT, cs] / l_fin[:, cs]).T
        oh = jnp.dot(o_lat.astype(BF16), wuv_ref[h], preferred_element_type=F32)
        g = g_ref[:, h * SPA_DV:(h + 1) * SPA_DV].astype(F32)
        o_ref[:, h * SPA_DV:(h + 1) * SPA_DV] = (oh * _silu(g)).astype(o_ref.dtype)


def _spa(hb, ht, wt, ikn, cn, cnt, wuv, tq, tk, topk):
    s = hb.shape[0]
    nq, nt = s // tq, s // tk
    const3 = lambda i: (0, 0, 0)
    return pl.pallas_call(
        functools.partial(_spa_kernel, tq=tq, tk=tk, topk=topk, idx_bits=(s - 1).bit_length()),
        out_shape=jax.ShapeDtypeStruct((s, SPA_HEADS * SPA_DV), BF16),
        grid=(nq,),
        in_specs=[
            pl.BlockSpec((IDX_HEADS * IDX_DIM, tq), lambda i: (HT_CIQ // (IDX_HEADS * IDX_DIM), i)),
            pl.BlockSpec((SPA_HEADS * SPA_LAT, tq), lambda i: (HT_CQ // (SPA_HEADS * SPA_LAT), i)),
            pl.BlockSpec((None, IDX_HEADS, tq), lambda i: (i, 0, 0)),
            pl.BlockSpec((tq, 1024), lambda i: (i, _HB_OFF["cg"] // 1024)),
            pl.BlockSpec((nt, tk, IDX_DIM), const3, pipeline_mode=pl.Buffered(1)),
            pl.BlockSpec((nt, tk, SPA_LAT), const3, pipeline_mode=pl.Buffered(1)),
            pl.BlockSpec((nt, SPA_LATP, tk), const3, pipeline_mode=pl.Buffered(1)),
            pl.BlockSpec((SPA_HEADS, SPA_LAT, SPA_DV), const3, pipeline_mode=pl.Buffered(1)),
        ],
        out_specs=pl.BlockSpec((tq, 1024), lambda i: (i, 0)),
        scratch_shapes=[pltpu.VMEM((nt + 1, tk, tq), I32), pltpu.VMEM((SPA_LATP, SPA_HEADS * tq), F32),
                        pltpu.VMEM((IDX_DIM, IDX_HEADS * tq), BF16), pltpu.VMEM((SPA_LAT, SPA_HEADS * tq), BF16),
                        pltpu.VMEM((tk, SPA_HEADS * tq), F32), pltpu.VMEM((tk, SPA_HEADS * tq), F32)],
        compiler_params=_cparams(("arbitrary",)),
        name="sparse_attn",
    )(ht, ht, wt, hb, ikn, cn, cnt, wuv)


def _out_kernel(oa_ref, ob_ref, oc_ref, x_ref, w_ref, lg_ref, lb_ref, y_ref, yb_ref, *, alpha):
    d = y_ref.shape[1]
    ncol = d // OUT_NC
    na, nb = oa_ref.shape[1], ob_ref.shape[1]
    oa, ob, oc = oa_ref[...], ob_ref[...], oc_ref[...]
    ssum = 0.0
    for c in range(ncol):
        cs = slice(c * OUT_NC, (c + 1) * OUT_NC)
        r = (alpha * x_ref[:, cs]
             + jnp.dot(oa, w_ref[0:na, cs], preferred_element_type=F32)
             + jnp.dot(ob, w_ref[na:na + nb, cs], preferred_element_type=F32)
             + jnp.dot(oc, w_ref[na + nb:, cs], preferred_element_type=F32))
        y_ref[:, cs] = r
        ssum = ssum + jnp.sum(r, axis=-1, keepdims=True)
    mu = ssum / d
    vsum = 0.0
    for c in range(ncol):
        cen = y_ref[:, c * OUT_NC:(c + 1) * OUT_NC] - mu
        vsum = vsum + jnp.sum(cen * cen, axis=-1, keepdims=True)
    rstd = lax.rsqrt(vsum / d + LN_EPS)
    for c in range(ncol):
        cs = slice(c * OUT_NC, (c + 1) * OUT_NC)
        y = (y_ref[:, cs] - mu) * rstd * lg_ref[:, cs] + lb_ref[:, cs]
        y_ref[:, cs] = y
        yb_ref[:, cs] = y.astype(BF16)


def _out_proj(oa, ob, oc, x, w, lg, lb, alpha, tm):
    s, d = x.shape
    row = lambda i: (i, 0)
    const = lambda i: (0, 0)
    return pl.pallas_call(
        functools.partial(_out_kernel, alpha=alpha),
        out_shape=(jax.ShapeDtypeStruct((s, d), F32), jax.ShapeDtypeStruct((s, d), BF16)),
        grid=(s // tm,),
        in_specs=[
            pl.BlockSpec((tm, oa.shape[1]), row),
            pl.BlockSpec((tm, ob.shape[1]), row),
            pl.BlockSpec((tm, oc.shape[1]), row),
            pl.BlockSpec((tm, d), row),
            pl.BlockSpec(w.shape, const, pipeline_mode=pl.Buffered(1)),
            pl.BlockSpec((1, d), const),
            pl.BlockSpec((1, d), const),
        ],
        out_specs=(pl.BlockSpec((tm, d), row), pl.BlockSpec((tm, d), row)),
        compiler_params=_cparams(("parallel",)),
        name="out_proj_ln",
    )(oa, ob, oc, x, w, lg, lb)


def kernel(x, w_in, w_out, gla_w_gate2, gla_b_gate, gla_norm_g, diff_lambda, diff_norm_g,
           spa_kv_norm_g, spa_ik_ln_g, spa_ik_ln_b, spa_w_uv, post_ln_g, post_ln_b):
    bsz, s, d = x.shape
    assert bsz == 1 and d == 4096 and w_in.shape[2] == sum(_IN_WIDTHS)
    depth = w_in.shape[0]
    alpha = (2.0 * depth) ** 0.25
    topk = min(IDX_TOPK_MAX, s // 4)
    tm_in = min(1024, s)
    t_diff = min(1024, s)
    tq_s, tk_s = 128, min(1024, s)
    nq_s = s // tq_s

    xf = x[0]
    xb = xf.astype(BF16)
    for l in range(depth):
        wb, wf, wqt = _prep_w_in(w_in[l])
        tn = IN_TN
        hb = _matmul(xb, wb, BF16, tm_in, tn)
        hf = _matmul(xb, wf, F32, tm_in, tn)
        ht = _matmul_t(wqt, xb, BF16, tn, tm_in,
                       fold=((HT_CQ // tn, (HT_CQ + _IN_OFF["cq"][1]) // tn, _Q_FOLD["cq"]),
                             (HT_BQ // tn, (HT_BQ + _IN_OFF["bq"][1]) // tn, _Q_FOLD["bq"])))

        o_a = _gla(hb, hf, gla_w_gate2[l], gla_b_gate[l][None], gla_norm_g[l][None], min(256, s))

        lam_init = 0.8 - 0.6 * math.exp(-0.3 * l)
        o_b = _diff(hb, ht, diff_lambda[l], diff_norm_g[l][None], lam_init, t_diff)

        cn, cnt, ikn = _spa_norm(hf, spa_kv_norm_g[l][None], spa_ik_ln_g[l][None], spa_ik_ln_b[l][None], tk_s)
        ciw = lax.slice_in_dim(hf, HF_SMALL + SM_CIW, HF_SMALL + SM_CIW + IDX_HEADS, axis=1)
        wt = (ciw * (IDX_HEADS ** -0.5 * IDX_DIM ** -0.5)).reshape(nq_s, tq_s, IDX_HEADS).transpose(0, 2, 1)
        o_c = _spa(hb, ht, wt, ikn, cn, cnt, spa_w_uv[l].astype(BF16), tq_s, tk_s, topk)

        xf, xb = _out_proj(o_a, o_b, o_c, xf, w_out[l].astype(BF16), post_ln_g[l][None], post_ln_b[l][None],
                           alpha, min(128, s))
    return xf[None]
```

```python
import functools
import math

import jax
import jax.numpy as jnp
from jax import lax
from jax.experimental import pallas as pl
from jax.experimental.pallas import tpu as pltpu

F32 = jnp.float32
BF16 = jnp.bfloat16
I32 = jnp.int32

CHUNK = 64
GLA_HEADS, GLA_DK, GLA_DV, GLA_RANK, GLA_TAU = 4, 256, 512, 16, 16.0
DIFF_HEADS, DIFF_DQK, DIFF_DV = 4, 128, 256
SPA_HEADS, SPA_DV, SPA_LAT = 8, 128, 256
IDX_HEADS, IDX_DIM, IDX_TOPK_MAX = 16, 64, 256
SPA_LATP = SPA_LAT + 16
LN_EPS, RMS_EPS = 1e-5, 1e-6

_IN_WIDTHS = (1024, 1024, 2048, 16, 2048, 1024, 1024, 1024, 1024, 2048, 256, 1024, 64, 16, 1024)
_IN_NAMES = ("aq", "ak", "av", "aa", "ag", "bq", "bk", "bv", "bg", "cq", "ckv", "ciq", "cik", "ciw", "cg")
_IN_OFF = {}
_acc = 0
for _n, _w in zip(_IN_NAMES, _IN_WIDTHS):
    _IN_OFF[_n] = (_acc, _w)
    _acc += _w

_HB_ORDER = ("av", "ag", "aq", "bk", "bg", "cg")
_HB_OFF = {}
_acc = 0
for _n in _HB_ORDER:
    _HB_OFF[_n] = _acc
    _acc += _IN_OFF[_n][1]
HB_WIDTH = _acc
_HT_ORDER = ("cq", "ciq", "bq", "bv")
HT_CQ, HT_CIQ, HT_BQ, HT_BV, HT_WIDTH = 0, 2048, 3072, 4096, 5120
HF_AK, HF_CKV, HF_SMALL, HF_WIDTH = 0, 1024, 1280, 1536
SM_CIK, SM_AA, SM_CIW = 0, 64, 80

_Q_FOLD = {"bq": DIFF_DQK ** -0.5 * math.log2(math.e), "cq": SPA_LAT ** -0.5 * math.log2(math.e)}
IN_TN = 1024
HF_TN = 512
COUNT_ROWS = 512
BISECT_BLIND_BITS = 24
OUT_NC = 1024
VMEM_LIMIT = 56 * 1024 * 1024
NEG_BIG = -1e30
INT_MIN = -(2 ** 31)


def _cparams(sem):
    return pltpu.CompilerParams(dimension_semantics=sem, vmem_limit_bytes=VMEM_LIMIT)


def _silu(g):
    return g / (1.0 + jnp.exp(-g))


def _tile_scale(folds):
    j = pl.program_id(1)
    scale = jnp.float32(1.0)
    for lo, hi, c in folds:
        scale = jnp.where(jnp.logical_and(j >= lo, j < hi), c, scale)
    return scale


def _matmul_kernel(x_ref, w_ref, o_ref, *, fold):
    acc = jnp.dot(x_ref[...], w_ref[...], preferred_element_type=F32)
    o_ref[...] = (acc * _tile_scale(fold)).astype(o_ref.dtype)


def _matmul(x, w, out_dtype, tm, tn, fold=()):
    m, k = x.shape
    n = w.shape[1]
    return pl.pallas_call(
        functools.partial(_matmul_kernel, fold=fold),
        out_shape=jax.ShapeDtypeStruct((m, n), out_dtype),
        grid=(m // tm, n // tn),
        in_specs=[pl.BlockSpec((tm, k), lambda i, j: (i, 0)),
                  pl.BlockSpec((k, tn), lambda i, j: (0, j))],
        out_specs=pl.BlockSpec((tm, tn), lambda i, j: (i, j)),
        compiler_params=_cparams(("parallel", "parallel")),
        name="in_proj",
    )(x, w)


def _matmul_t_kernel(wt_ref, x_ref, o_ref, *, fold):
    acc = lax.dot_general(wt_ref[...], x_ref[...], (((1,), (1,)), ((), ())), preferred_element_type=F32)
    o_ref[...] = (acc * _tile_scale(fold)).astype(o_ref.dtype)


def _matmul_t(wt, x, out_dtype, tn, tm, fold=()):
    n, k = wt.shape
    m = x.shape[0]
    return pl.pallas_call(
        functools.partial(_matmul_t_kernel, fold=fold),
        out_shape=jax.ShapeDtypeStruct((n, m), out_dtype),
        grid=(m // tm, n // tn),
        in_specs=[pl.BlockSpec((tn, k), lambda i, j: (j, 0)),
                  pl.BlockSpec((tm, k), lambda i, j: (i, 0))],
        out_specs=pl.BlockSpec((tn, tm), lambda i, j: (j, i)),
        compiler_params=_cparams(("parallel", "parallel")),
        name="in_proj_t",
    )(wt, x)


def _prep_w_in(w):
    def cols(name):
        o, n = _IN_OFF[name]
        return w[:, o:o + n]
    d = w.shape[0]
    wb = jnp.concatenate([cols(n) for n in _HB_ORDER], axis=1).astype(BF16)
    small = jnp.concatenate([cols("cik"), cols("aa"), cols("ciw"), jnp.zeros((d, 32), w.dtype)], axis=1)
    wf = jnp.concatenate([cols("ak"), cols("ckv"), small, jnp.zeros((d, HF_WIDTH - HF_SMALL - 128), w.dtype)],
                         axis=1).astype(BF16)
    wt = jnp.concatenate([cols(n) for n in _HT_ORDER], axis=1).astype(BF16).T
    return wb, wf, wt


def _gla_kernel(v_ref, q_ref, g_ref, k_ref, sm_ref, w2_ref, b_ref, ng_ref, o_ref, st_ref, *, nchunk):
    @pl.when(pl.program_id(0) == 0)
    def _():
        st_ref[...] = jnp.zeros_like(st_ref)

    row = lax.broadcasted_iota(I32, (CHUNK, CHUNK), 0)
    col = lax.broadcasted_iota(I32, (CHUNK, CHUNK), 1)
    tri = (col <= row).astype(F32)
    chunks = [slice(c * CHUNK, (c + 1) * CHUNK) for c in range(nchunk)]
    kss = [slice(h * GLA_DK, (h + 1) * GLA_DK) for h in range(GLA_HEADS)]
    vss = [slice(h * GLA_DV, (h + 1) * GLA_DV) for h in range(GLA_HEADS)]

    aa = sm_ref[:, SM_AA:SM_AA + GLA_RANK].astype(BF16)
    z = jnp.dot(aa, w2_ref[...].astype(BF16), preferred_element_type=F32) + b_ref[...]
    log_a = (jnp.minimum(z, 0.0) - jnp.log1p(jnp.exp(-jnp.abs(z)))) * (1.0 / GLA_TAU)
    cums = [jnp.dot(tri, log_a[rs, :], preferred_element_type=F32, precision=lax.Precision.HIGHEST)
            for rs in chunks]
    tots = [cum[CHUNK - 1:CHUNK, :] for cum in cums]
    k_decs = [(k_ref[rs, :] * jnp.exp(tot - cum)).astype(BF16) for rs, cum, tot in zip(chunks, cums, tots)]
    decays = [jnp.exp(tot) for tot in tots]
    upds = [[lax.dot_general(v_ref[rs, vss[h]], k_dec[:, kss[h]], (((0,), (0,)), ((), ())),
                             preferred_element_type=F32) for h in range(GLA_HEADS)]
            for rs, k_dec in zip(chunks, k_decs)]
    states = [[None] * GLA_HEADS for _ in range(nchunk)]
    for h in range(GLA_HEADS):
        st = st_ref[h]
        for c in range(nchunk):
            st = st * decays[c][:, kss[h]] + upds[c][h]
            states[c][h] = st.astype(BF16)
        st_ref[h] = st
    outs = [[lax.dot_general(q_ref[rs, kss[h]], states[c][h], (((1,), (1,)), ((), ())),
                             preferred_element_type=F32) for h in range(GLA_HEADS)]
            for c, rs in enumerate(chunks)]
    for c, rs in enumerate(chunks):
        for h in range(GLA_HEADS):
            o = outs[c][h] * (GLA_DK ** -0.5)
            var = jnp.mean(o * o, axis=-1, keepdims=True)
            y = o * lax.rsqrt(var + RMS_EPS) * ng_ref[...]
            g = g_ref[rs, vss[h]].astype(F32)
            o_ref[rs, vss[h]] = (y * _silu(g)).astype(o_ref.dtype)


def _gla(hb, hf, w2, b, ng, tr):
    s = hb.shape[0]
    nchunk = tr // CHUNK
    return pl.pallas_call(
        functools.partial(_gla_kernel, nchunk=nchunk),
        out_shape=jax.ShapeDtypeStruct((s, GLA_HEADS * GLA_DV), BF16),
        grid=(s // tr,),
        in_specs=[
            pl.BlockSpec((tr, 2048), lambda i: (i, _HB_OFF["av"] // 2048)),
            pl.BlockSpec((tr, 1024), lambda i: (i, _HB_OFF["aq"] // 1024)),
            pl.BlockSpec((tr, 2048), lambda i: (i, _HB_OFF["ag"] // 2048)),
            pl.BlockSpec((tr, 1024), lambda i: (i, HF_AK // 1024)),
            pl.BlockSpec((tr, 128), lambda i: (i, HF_SMALL // 128)),
            pl.BlockSpec((GLA_RANK, 1024), lambda i: (0, 0)),
            pl.BlockSpec((1, 1024), lambda i: (0, 0)),
            pl.BlockSpec((1, GLA_DV), lambda i: (0, 0)),
        ],
        out_specs=pl.BlockSpec((tr, 2048), lambda i: (i, 0)),
        scratch_shapes=[pltpu.VMEM((GLA_HEADS, GLA_DV, GLA_DK), F32)],
        compiler_params=_cparams(("arbitrary",)),
        name="gla",
    )(hb, hb, hb, hf, hf, w2, b, ng)


def _diff_kernel(lam_ref, q_ref, k_ref, v_ref, g_ref, ng_ref, o_ref, acc_ref, sa_ref, sb_ref, *, t, lam_init):
    i = pl.program_id(1)
    lp = lam_ref[...]
    lam = (jnp.exp(jnp.sum(lp[0:1] * lp[1:2], axis=(0, 1), keepdims=True))
           - jnp.exp(jnp.sum(lp[2:3] * lp[3:4], axis=(0, 1), keepdims=True)) + lam_init)
    acc_ref[...] = jnp.zeros_like(acc_ref)
    maps = [slice(mp * DIFF_DQK, (mp + 1) * DIFF_DQK) for mp in range(2)]

    def qk(j, s_ref):
        kt = k_ref[pl.ds(pl.multiple_of(j * t, t), t), :]
        for mp in range(2):
            s_ref[mp] = jnp.dot(kt[:, maps[mp]], q_ref[maps[mp], :], preferred_element_type=F32)

    def soft_pv(j, s_ref, carry, masked):
        vt = v_ref[:, pl.ds(pl.multiple_of(j * t, t), t)]
        out = []
        for mp in range(2):
            m_old, l_old = carry[mp]
            s = s_ref[mp]
            if masked:
                kch = lax.broadcasted_iota(I32, (t, t), 0) // CHUNK
                qch = lax.broadcasted_iota(I32, (t, t), 1) // CHUNK
                s = jnp.where(kch <= qch, s, NEG_BIG)
            m_new = jnp.maximum(m_old, jnp.max(s, axis=0, keepdims=True))
            p = jnp.exp2(s - m_new)
            alpha = jnp.exp2(m_old - m_new)
            l_new = alpha * l_old + jnp.sum(p, axis=0, keepdims=True)
            acc_ref[mp] = acc_ref[mp] * alpha + jnp.dot(vt, p.astype(BF16), preferred_element_type=F32)
            out.append((m_new, l_new))
        return tuple(out)

    def pair(jj, carry):
        j0 = 2 * jj
        qk(j0 + 1, sb_ref)
        carry = soft_pv(j0, sa_ref, carry, False)
        qk(j0 + 2, sa_ref)
        return soft_pv(j0 + 1, sb_ref, carry, False)

    def even_tail(carry):
        return soft_pv(i, sa_ref, carry, True)

    def odd_tail(carry):
        qk(i, sb_ref)
        carry = soft_pv(i - 1, sa_ref, carry, False)
        return soft_pv(i, sb_ref, carry, True)

    qk(0, sa_ref)
    init = tuple((jnp.full((1, t), NEG_BIG, F32), jnp.zeros((1, t), F32)) for _ in range(2))
    (_, l0), (_, l1) = lax.cond(i % 2 == 0, even_tail, odd_tail, lax.fori_loop(0, i // 2, pair, init))
    o = (acc_ref[0] / l0 - lam * (acc_ref[1] / l1)).T
    var = jnp.mean(o * o, axis=-1, keepdims=True)
    y = o * lax.rsqrt(var + RMS_EPS) * ng_ref[...] * (1.0 - lam_init)
    g = g_ref[...].astype(F32)
    o_ref[...] = (y * _silu(g)).astype(o_ref.dtype)


def _diff(hb, ht, lam_p, ng, lam_init, t):
    s = hb.shape[0]
    w = DIFF_DV
    return pl.pallas_call(
        functools.partial(_diff_kernel, t=t, lam_init=lam_init),
        out_shape=jax.ShapeDtypeStruct((s, DIFF_HEADS * DIFF_DV), BF16),
        grid=(DIFF_HEADS, s // t),
        in_specs=[
            pl.BlockSpec((4, DIFF_DQK), lambda h, i: (0, 0)),
            pl.BlockSpec((w, t), lambda h, i: (HT_BQ // w + h, i)),
            pl.BlockSpec((s, w), lambda h, i: (0, _HB_OFF["bk"] // w + h), pipeline_mode=pl.Buffered(1)),
            pl.BlockSpec((w, s), lambda h, i: (HT_BV // w + h, 0), pipeline_mode=pl.Buffered(1)),
            pl.BlockSpec((t, w), lambda h, i: (i, _HB_OFF["bg"] // w + h)),
            pl.BlockSpec((1, DIFF_DV), lambda h, i: (0, 0)),
        ],
        out_specs=pl.BlockSpec((t, w), lambda h, i: (i, h)),
        scratch_shapes=[pltpu.VMEM((2, DIFF_DV, t), F32), pltpu.VMEM((2, t, t), F32), pltpu.VMEM((2, t, t), F32)],
        compiler_params=_cparams(("parallel", "arbitrary")),
        name="diff_attn",
    )(lam_p, ht, hb, ht, hb, ng)


def _spa_norm_kernel(ckv_ref, sm_ref, kvg_ref, lg_ref, lb_ref, cn_ref, cnt_ref, ikn_ref):
    c = ckv_ref[...]
    cn = c * lax.rsqrt(jnp.mean(c * c, axis=-1, keepdims=True) + RMS_EPS) * kvg_ref[...]
    cn_ref[...] = cn.astype(cn_ref.dtype)
    cnt_ref[0:SPA_LAT, :] = cn.T.astype(cnt_ref.dtype)
    pad_rows = cnt_ref.shape[0] - SPA_LAT
    ones_row = lax.broadcasted_iota(I32, (pad_rows, cnt_ref.shape[1]), 0) == 0
    cnt_ref[SPA_LAT:, :] = jnp.where(ones_row, 1.0, 0.0).astype(cnt_ref.dtype)
    ik = sm_ref[:, SM_CIK:SM_CIK + IDX_DIM]
    mu = jnp.mean(ik, axis=-1, keepdims=True)
    d = ik - mu
    var = jnp.mean(d * d, axis=-1, keepdims=True)
    ikn_ref[...] = (d * lax.rsqrt(var + LN_EPS) * lg_ref[...] + lb_ref[...]).astype(ikn_ref.dtype)


def _spa_norm(hf, kvg, lg, lb, tk):
    s = hf.shape[0]
    nt = s // tk
    return pl.pallas_call(
        _spa_norm_kernel,
        out_shape=(jax.ShapeDtypeStruct((nt, tk, SPA_LAT), BF16),
                   jax.ShapeDtypeStruct((nt, SPA_LATP, tk), BF16),
                   jax.ShapeDtypeStruct((nt, tk, IDX_DIM), BF16)),
        grid=(nt,),
        in_specs=[
            pl.BlockSpec((tk, SPA_LAT), lambda i: (i, HF_CKV // SPA_LAT)),
            pl.BlockSpec((tk, 128), lambda i: (i, HF_SMALL // 128)),
            pl.BlockSpec((1, SPA_LAT), lambda i: (0, 0)),
            pl.BlockSpec((1, IDX_DIM), lambda i: (0, 0)),
            pl.BlockSpec((1, IDX_DIM), lambda i: (0, 0)),
        ],
        out_specs=(pl.BlockSpec((None, tk, SPA_LAT), lambda i: (i, 0, 0)),
                   pl.BlockSpec((None, SPA_LATP, tk), lambda i: (i, 0, 0)),
                   pl.BlockSpec((None, tk, IDX_DIM), lambda i: (i, 0, 0))),
        compiler_params=_cparams(("parallel",)),
        name="spa_norm",
    )(hf, hf, kvg, lg, lb)


def _spa_kernel(iq_ref, q8_ref, w_ref, g_ref, ikn_ref, cn_ref, cnt_ref, wuv_ref, o_ref, keys_ref, acc_ref,
                iqt_ref, q8t_ref, sa_ref, sb_ref, *, tq, tk, topk, idx_bits):
    i = pl.program_id(0)
    for h in range(IDX_HEADS):
        iqt_ref[:, h * tq:(h + 1) * tq] = iq_ref[h * IDX_DIM:(h + 1) * IDX_DIM, :]
    for h in range(SPA_HEADS):
        q8t_ref[:, h * tq:(h + 1) * tq] = q8_ref[h * SPA_LAT:(h + 1) * SPA_LAT, :]
    n_t = (i * tq + tq + tk - 1) // tk
    qpos = i * tq + lax.broadcasted_iota(I32, (1, tq), 1)
    qch = qpos // CHUNK
    k_row = jnp.minimum(topk, (qch + 1) * CHUNK)
    hk = tk // 2

    def score_tile(j, masked):
        for half in range(2):
            ks = slice(half * hk, (half + 1) * hk)
            ikt = ikn_ref[j, ks, :]
            sc = jnp.zeros((hk, tq), F32)
            for hp in range(IDX_HEADS // 2):
                r = jnp.dot(ikt, iqt_ref[:, hp * 2 * tq:(hp + 1) * 2 * tq], preferred_element_type=F32)
                for hh in range(2):
                    h = hp * 2 + hh
                    sc = sc + jnp.maximum(r[:, hh * tq:(hh + 1) * tq], 0.0) * w_ref[h:h + 1, :]
            bits = pltpu.bitcast(sc, I32)
            key = bits ^ ((bits >> 31) & 0x7FFFFFFF)
            if masked:
                kch = (j * tk + half * hk + lax.broadcasted_iota(I32, (hk, tq), 0)) // CHUNK
                key = jnp.where(kch <= qch, key, INT_MIN)
            keys_ref[j, ks, :] = key

    def body_a(j, _):
        score_tile(j, False)
        return 0

    lax.fori_loop(0, n_t - 1, body_a, 0)
    score_tile(n_t - 1, True)

    keys_ref[n_t] = jnp.full((tk, tq), INT_MIN, I32)
    n_t2 = (n_t + 1) // 2

    cb = min(COUNT_ROWS, tk)

    def count_where(pred):
        def body(jj, cnt):
            for j in (2 * jj, 2 * jj + 1):
                for r0 in range(0, tk, cb):
                    ind = jnp.where(pred(keys_ref[j, r0:r0 + cb, :], j * tk + r0), 1, 0).astype(I32)
                    cnt = cnt + jnp.sum(ind.reshape(cb // 8, 8, tq), axis=0)
            return cnt
        cnt = lax.fori_loop(0, n_t2, body, jnp.zeros((8, tq), I32))
        return jnp.sum(cnt, axis=0, keepdims=True)

    def bis_cond(st):
        b, _, cnt_a = st
        return jnp.logical_and(b < 32, jnp.max(cnt_a - k_row) > 0)

    def bis_body(st):
        b, a, cnt_a = st
        cand_a = a | lax.shift_left(jnp.int32(1), 31 - b)
        cand = cand_a ^ INT_MIN
        cnt = count_where(lambda kt, first: kt >= cand)
        take = cnt >= k_row
        return b + 1, jnp.where(take, cand_a, a), jnp.where(take, cnt, cnt_a)

    st0 = (jnp.int32(0), jnp.zeros((1, tq), I32), jnp.zeros((1, tq), I32) + n_t * tk)
    st1 = lax.fori_loop(0, BISECT_BLIND_BITS, lambda _, st: bis_body(st), st0)
    _, a_fin, cnt_fin = lax.while_loop(bis_cond, bis_body, st1)
    thr = a_fin ^ INT_MIN

    @pl.when(jnp.max(cnt_fin - k_row) > 0)
    def _():
        need = k_row - count_where(lambda kt, first: kt > thr)

        def idx_of(first, rows):
            return first + lax.broadcasted_iota(I32, (rows, tq), 0)

        def idx_body(b, y):
            cand = y | lax.shift_left(jnp.int32(1), idx_bits - 1 - b)
            cnt = count_where(lambda kt, first: jnp.logical_and(kt == thr, idx_of(first, cb) < cand))
            return jnp.where(cnt < need, cand, y)

        y_keep = lax.fori_loop(0, idx_bits, idx_body, jnp.zeros((1, tq), I32))

        def fix_body(j, _):
            kt = keys_ref[j]
            drop = jnp.logical_and(kt == thr, idx_of(j * tk, tk) > y_keep)
            keys_ref[j] = jnp.where(drop, kt - 1, kt)
            return 0

        lax.fori_loop(0, n_t, fix_body, 0)

    acc_ref[...] = jnp.zeros_like(acc_ref)

    last = keys_ref.shape[0] - 2

    def qk(j, s_ref):
        s_ref[...] = jnp.dot(cn_ref[jnp.minimum(j, last)], q8t_ref[...], preferred_element_type=F32)

    def soft_pv(j, s_ref, m_old):
        madd = jnp.where(keys_ref[j] >= thr, 0.0, NEG_BIG)
        s = s_ref[...] + jnp.concatenate([madd] * SPA_HEADS, axis=1)
        m_new = jnp.maximum(m_old, jnp.max(s, axis=0, keepdims=True))
        p = jnp.exp2(s - m_new)
        alpha = jnp.exp2(m_old - m_new)
        acc_ref[...] = acc_ref[...] * alpha + jnp.dot(cnt_ref[jnp.minimum(j, last)], p.astype(BF16),
                                                      preferred_element_type=F32)
        return m_new

    def body_c(jj, m_run):
        j0 = 2 * jj
        qk(j0 + 1, sb_ref)
        m_run = soft_pv(j0, sa_ref, m_run)
        qk(j0 + 2, sa_ref)
        return soft_pv(j0 + 1, sb_ref, m_run)

    qk(0, sa_ref)
    m_run = lax.fori_loop(0, n_t // 2, body_c, jnp.full((1, SPA_HEADS * tq), NEG_BIG, F32))

    @pl.when(n_t % 2 == 1)
    def _():
        soft_pv(n_t - 1, sa_ref, m_run)

    l_fin = acc_ref[SPA_LAT:SPA_LAT + 1, :]

    for h in range(SPA_HEADS):
        cs = slice(h * tq, (h + 1) * tq)
        o_lat = (acc_ref[0:SPA_LAT, cs] / l_fin[:, cs]).T
        oh = jnp.dot(o_lat.astype(BF16), wuv_ref[h], preferred_element_type=F32)
        g = g_ref[:, h * SPA_DV:(h + 1) * SPA_DV].astype(F32)
        o_ref[:, h * SPA_DV:(h + 1) * SPA_DV] = (oh * _silu(g)).astype(o_ref.dtype)


def _spa(hb, ht, wt, ikn, cn, cnt, wuv, tq, tk, topk):
    s = hb.shape[0]
    nq, nt = s // tq, s // tk
    const3 = lambda i: (0, 0, 0)
    return pl.pallas_call(
        functools.partial(_spa_kernel, tq=tq, tk=tk, topk=topk, idx_bits=(s - 1).bit_length()),
        out_shape=jax.ShapeDtypeStruct((s, SPA_HEADS * SPA_DV), BF16),
        grid=(nq,),
        in_specs=[
            pl.BlockSpec((IDX_HEADS * IDX_DIM, tq), lambda i: (HT_CIQ // (IDX_HEADS * IDX_DIM), i)),
            pl.BlockSpec((SPA_HEADS * SPA_LAT, tq), lambda i: (HT_CQ // (SPA_HEADS * SPA_LAT), i)),
            pl.BlockSpec((None, IDX_HEADS, tq), lambda i: (i, 0, 0)),
            pl.BlockSpec((tq, 1024), lambda i: (i, _HB_OFF["cg"] // 1024)),
            pl.BlockSpec((nt, tk, IDX_DIM), const3, pipeline_mode=pl.Buffered(1)),
            pl.BlockSpec((nt, tk, SPA_LAT), const3, pipeline_mode=pl.Buffered(1)),
            pl.BlockSpec((nt, SPA_LATP, tk), const3, pipeline_mode=pl.Buffered(1)),
            pl.BlockSpec((SPA_HEADS, SPA_LAT, SPA_DV), const3, pipeline_mode=pl.Buffered(1)),
        ],
        out_specs=pl.BlockSpec((tq, 1024), lambda i: (i, 0)),
        scratch_shapes=[pltpu.VMEM((nt + 1, tk, tq), I32), pltpu.VMEM((SPA_LATP, SPA_HEADS * tq), F32),
                        pltpu.VMEM((IDX_DIM, IDX_HEADS * tq), BF16), pltpu.VMEM((SPA_LAT, SPA_HEADS * tq), BF16),
                        pltpu.VMEM((tk, SPA_HEADS * tq), F32), pltpu.VMEM((tk, SPA_HEADS * tq), F32)],
        compiler_params=_cparams(("arbitrary",)),
        name="sparse_attn",
    )(ht, ht, wt, hb, ikn, cn, cnt, wuv)


def _out_kernel(oa_ref, ob_ref, oc_ref, x_ref, w_ref, lg_ref, lb_ref, y_ref, yb_ref, *, alpha):
    d = y_ref.shape[1]
    ncol = d // OUT_NC
    na, nb = oa_ref.shape[1], ob_ref.shape[1]
    oa, ob, oc = oa_ref[...], ob_ref[...], oc_ref[...]
    ssum = 0.0
    for c in range(ncol):
        cs = slice(c * OUT_NC, (c + 1) * OUT_NC)
        r = (alpha * x_ref[:, cs]
             + jnp.dot(oa, w_ref[0:na, cs], preferred_element_type=F32)
             + jnp.dot(ob, w_ref[na:na + nb, cs], preferred_element_type=F32)
             + jnp.dot(oc, w_ref[na + nb:, cs], preferred_element_type=F32))
        y_ref[:, cs] = r
        ssum = ssum + jnp.sum(r, axis=-1, keepdims=True)
    mu = ssum / d
    vsum = 0.0
    for c in range(ncol):
        cen = y_ref[:, c * OUT_NC:(c + 1) * OUT_NC] - mu
        vsum = vsum + jnp.sum(cen * cen, axis=-1, keepdims=True)
    rstd = lax.rsqrt(vsum / d + LN_EPS)
    for c in range(ncol):
        cs = slice(c * OUT_NC, (c + 1) * OUT_NC)
        y = (y_ref[:, cs] - mu) * rstd * lg_ref[:, cs] + lb_ref[:, cs]
        y_ref[:, cs] = y
        yb_ref[:, cs] = y.astype(BF16)


def _out_proj(oa, ob, oc, x, w, lg, lb, alpha, tm):
    s, d = x.shape
    row = lambda i: (i, 0)
    const = lambda i: (0, 0)
    return pl.pallas_call(
        functools.partial(_out_kernel, alpha=alpha),
        out_shape=(jax.ShapeDtypeStruct((s, d), F32), jax.ShapeDtypeStruct((s, d), BF16)),
        grid=(s // tm,),
        in_specs=[
            pl.BlockSpec((tm, oa.shape[1]), row),
            pl.BlockSpec((tm, ob.shape[1]), row),
            pl.BlockSpec((tm, oc.shape[1]), row),
            pl.BlockSpec((tm, d), row),
            pl.BlockSpec(w.shape, const, pipeline_mode=pl.Buffered(1)),
            pl.BlockSpec((1, d), const),
            pl.BlockSpec((1, d), const),
        ],
        out_specs=(pl.BlockSpec((tm, d), row), pl.BlockSpec((tm, d), row)),
        compiler_params=_cparams(("parallel",)),
        name="out_proj_ln",
    )(oa, ob, oc, x, w, lg, lb)


def kernel(x, w_in, w_out, gla_w_gate2, gla_b_gate, gla_norm_g, diff_lambda, diff_norm_g,
           spa_kv_norm_g, spa_ik_ln_g, spa_ik_ln_b, spa_w_uv, post_ln_g, post_ln_b):
    bsz, s, d = x.shape
    assert bsz == 1 and d == 4096 and w_in.shape[2] == sum(_IN_WIDTHS)
    depth = w_in.shape[0]
    alpha = (2.0 * depth) ** 0.25
    topk = min(IDX_TOPK_MAX, s // 4)
    tm_in = min(1024, s)
    t_diff = min(1024, s)
    tq_s, tk_s = 128, min(1024, s)
    nq_s = s // tq_s

    xf = x[0]
    xb = xf.astype(BF16)
    for l in range(depth):
        wb, wf, wqt = _prep_w_in(w_in[l])
        tn = IN_TN
        hb = _matmul(xb, wb, BF16, tm_in, tn)
        hf = _matmul(xb, wf, F32, tm_in, HF_TN)
        ht = _matmul_t(wqt, xb, BF16, tn, tm_in,
                       fold=((HT_CQ // tn, (HT_CQ + _IN_OFF["cq"][1]) // tn, _Q_FOLD["cq"]),
                             (HT_BQ // tn, (HT_BQ + _IN_OFF["bq"][1]) // tn, _Q_FOLD["bq"])))

        o_a = _gla(hb, hf, gla_w_gate2[l], gla_b_gate[l][None], gla_norm_g[l][None], min(256, s))

        lam_init = 0.8 - 0.6 * math.exp(-0.3 * l)
        o_b = _diff(hb, ht, diff_lambda[l], diff_norm_g[l][None], lam_init, t_diff)

        cn, cnt, ikn = _spa_norm(hf, spa_kv_norm_g[l][None], spa_ik_ln_g[l][None], spa_ik_ln_b[l][None], tk_s)
        ciw = lax.slice_in_dim(hf, HF_SMALL + SM_CIW, HF_SMALL + SM_CIW + IDX_HEADS, axis=1)
        wt = (ciw * (IDX_HEADS ** -0.5 * IDX_DIM ** -0.5)).reshape(nq_s, tq_s, IDX_HEADS).transpose(0, 2, 1)
        o_c = _spa(hb, ht, wt, ikn, cn, cnt, spa_w_uv[l].astype(BF16), tq_s, tk_s, topk)

        xf, xb = _out_proj(o_a, o_b, o_c, xf, w_out[l].astype(BF16), post_ln_g[l][None], post_ln_b[l][None],
                           alpha, min(128, s))
    return xf[None]
```
